```python
import math
import jax, jax.numpy as jnp
from jax import lax
import numpy as np

D_MODEL = 1024
BATCH = 2
SEQ = 16384
DEPTH = 4

CTX_LEN = 256
GRID_W = 64
EPS = 1e-6
SSM_WIDTH = 512
SSM_GROUP = 16
SSM_GROUPS = SSM_WIDTH // SSM_GROUP
SSM_STATE = 64
DT_MIN = 0.001
DT_MAX = 0.1
RET_WIDTH = 512
RET_HEADS = 4
RET_HEAD_DIM = RET_WIDTH // RET_HEADS
RET_CHUNK = 128
ROPE_BASE = 10000.0
PEER_HEADS = 8
PEER_NKEYS = 128
PEER_EXPERTS = PEER_NKEYS * PEER_NKEYS
PEER_DKEY = 128
PEER_TOPK = 16
PEER_BLOCK = 128
IN_COLS = SSM_WIDTH + 4 * RET_WIDTH + 2 * D_MODEL
IN_SPLITS = (SSM_WIDTH, SSM_WIDTH + RET_WIDTH, SSM_WIDTH + 2 * RET_WIDTH,
             SSM_WIDTH + 3 * RET_WIDTH, SSM_WIDTH + 4 * RET_WIDTH,
             SSM_WIDTH + 4 * RET_WIDTH + D_MODEL)

kernel_name = "hybrid_s5_retention_peer_diffusion_trunk"


def _rmsnorm(x, g):
    xf = x.astype(jnp.float32)
    y = xf * lax.rsqrt(jnp.mean(xf * xf, axis=-1, keepdims=True) + EPS)
    return (y * g.astype(jnp.float32)).astype(x.dtype)


def _modulate(h, shift, scale):
    return h * (1.0 + scale) + shift


def _s5_discretise(lam_re, lam_im, log_dt):
    lam = lax.complex(lam_re.astype(jnp.float32), lam_im.astype(jnp.float32))
    dt = jnp.exp(log_dt.astype(jnp.float32))[:, None]
    lam_bar = jnp.exp(lam * dt)
    b_fac = (lam_bar - 1.0) / lam
    return lam_bar, b_fac


def _ssm_combine(e1, e2):
    a1, b1 = e1
    a2, b2 = e2
    return a1 * a2, a2 * b1 + b2


def _s5_scan(bu, lam_bar, b_fac, h0):
    b = bu * b_fac
    if h0 is not None:
        b = b.at[:, 0].add(lam_bar * h0)
    a = jnp.broadcast_to(lam_bar, b.shape)
    _, h = lax.associative_scan(_ssm_combine, (a, b), axis=1)
    return h


def _s5(u, uc, B_re, B_im, C_re, C_im, d_skip, lam_re_f, lam_im_f, log_dt_f,
        lam_re_b, lam_im_b, log_dt_b, with_ctx_out):
    f32 = jnp.float32
    b_in = lax.complex(B_re.astype(f32), B_im.astype(f32))
    c_out = lax.complex(C_re.astype(f32), C_im.astype(f32))
    d = d_skip.astype(f32).reshape(SSM_GROUPS, SSM_GROUP)

    def grouped(t):
        return t.astype(f32).reshape(t.shape[0], t.shape[1], SSM_GROUPS, SSM_GROUP)

    def drive(tg):
        return jnp.einsum('blgi,gpi->blgp', tg.astype(jnp.complex64), b_in)

    def readout(h, tg):
        y = jnp.einsum('blgp,gip->blgi', h, c_out).real + d * tg
        return y.reshape(y.shape[0], y.shape[1], SSM_WIDTH)

    fwd = _s5_discretise(lam_re_f, lam_im_f, log_dt_f)
    bwd = _s5_discretise(lam_re_b, lam_im_b, log_dt_b)
    flip = lambda t: jnp.flip(t, axis=1)
    ug, ucg = grouped(u), grouped(uc)
    bu, bu_c = drive(ug), drive(ucg)
    hc_f = _s5_scan(bu_c, *fwd, None)
    hc_b = flip(_s5_scan(flip(bu_c), *bwd, None))
    h_f = _s5_scan(bu, *fwd, hc_f[:, -1])
    h_b = flip(_s5_scan(flip(bu), *bwd, hc_b[:, 0]))
    y = readout(h_f + h_b, ug)
    yc = readout(hc_f + hc_b, ucg) if with_ctx_out else None
    return y, yc


def _rotate(t, cos, sin):
    t1, t2 = jnp.split(t, 2, axis=-1)
    return jnp.concatenate([t1 * cos - t2 * sin, t1 * sin + t2 * cos], axis=-1)


def _rope2d(t, cos_r, sin_r, cos_c, sin_c):
    tf = t.astype(jnp.float32)
    tr, tc = jnp.split(tf, 2, axis=-1)
    out = jnp.concatenate([_rotate(tr, cos_r, sin_r), _rotate(tc, cos_c, sin_c)], axis=-1)
    return out.astype(t.dtype)


def _heads(t):
    nb, l, _ = t.shape
    return t.reshape(nb, l, RET_HEADS, RET_HEAD_DIM).transpose(0, 2, 1, 3)


def _retention_dir(q, k, v, log_gamma, state0, strict, with_out):
    f32 = jnp.float32
    nb, nh, l, dh = q.shape
    nc = l // RET_CHUNK
    blk = lambda t: t.astype(f32).reshape(nb, nh, nc, RET_CHUNK, dh)
    qb, kb, vb = blk(q), blk(k), blk(v)
    pos = jnp.arange(RET_CHUNK, dtype=f32)
    lg = log_gamma[:, None]
    k_decay = jnp.exp((RET_CHUNK - 1.0 - pos)[None] * lg)
    kv = jnp.einsum('bhncd,bhnce,hc->bhnde', kb, vb, k_decay)
    chunk_decay = jnp.exp(RET_CHUNK * log_gamma)[None, :, None, None]

    def step(s, kv_n):
        return chunk_decay * s + kv_n, s

    final, s_before = lax.scan(step, state0, jnp.moveaxis(kv, 2, 0))
    if not with_out:
        return None, final
    s_before = jnp.moveaxis(s_before, 0, 2)
    diff = pos[:, None] - pos[None, :]
    mask = diff > 0 if strict else diff >= 0
    dmat = jnp.where(mask[None], jnp.exp(jnp.where(mask, diff, 0.0)[None] * log_gamma[:, None, None]), 0.0)
    scores = jnp.einsum('bhncd,bhnmd->bhncm', qb, kb) * dmat[None, :, None]
    inner = jnp.einsum('bhncm,bhnme->bhnce', scores, vb)
    q_decay = jnp.exp((pos + 1.0)[None] * lg)
    cross = jnp.einsum('bhncd,bhnde,hc->bhnce', qb, s_before, q_decay)
    return (inner + cross).reshape(nb, nh, l, dh), final


def _retention(q, k, v, qc, kc, vc, lg_f, lg_b, with_ctx_out):
    nb, nh, _, dh = q.shape
    zero = jnp.zeros((nb, nh, dh, dh), jnp.float32)
    flip = lambda t: jnp.flip(t, axis=2)
    oc_f, sc_f = _retention_dir(qc, kc, vc, lg_f, zero, False, with_ctx_out)
    oc_b, sc_b = _retention_dir(flip(qc), flip(kc), flip(vc), lg_b, zero, True, with_ctx_out)
    o_f, _ = _retention_dir(q, k, v, lg_f, sc_f, False, True)
    o_b, _ = _retention_dir(flip(q), flip(k), flip(v), lg_b, sc_b, True, True)
    o = o_f + flip(o_b)
    oc = oc_f + flip(oc_b) if with_ctx_out else None
    return o, oc


def _ret_out(o, g):
    nb, nh, l, dh = o.shape
    of = o.transpose(0, 2, 1, 3)
    of = of * lax.rsqrt(jnp.mean(of * of, axis=-1, keepdims=True) + EPS)
    return of.reshape(nb, l, nh * dh).astype(g.dtype) * jax.nn.silu(g)


def _merge(y_s, y_r, gate_s, gate_r, w_ssm_glu, w_ret_up, w_out):
    a, b = jnp.split(jax.nn.gelu(y_s) @ w_ssm_glu, 2, axis=-1)
    ys = a * jax.nn.sigmoid(b)
    yr = y_r @ w_ret_up
    m = jax.nn.sigmoid(gate_s) * ys + jax.nn.sigmoid(gate_r) * yr
    return m @ w_out


def _mixer(h, hc, rope, w_in, B_re, B_im, C_re, C_im, d_skip, lam_re_f, lam_im_f, log_dt_f,
           lam_re_b, lam_im_b, log_dt_b, w_ssm_glu, ret_decay_f, ret_decay_b, w_ret_up, w_out,
           with_ctx_out):
    u, q, k, v, g, gs, gr = jnp.split(h @ w_in, IN_SPLITS, axis=-1)
    uc, qc, kc, vc, gc, gsc, grc = jnp.split(hc @ w_in, IN_SPLITS, axis=-1)
    ys, ysc = _s5(u, uc, B_re, B_im, C_re, C_im, d_skip, lam_re_f, lam_im_f, log_dt_f,
                  lam_re_b, lam_im_b, log_dt_b, with_ctx_out)
    k_scale = RET_HEAD_DIM ** -0.5
    q_h = _rope2d(_heads(q), *rope)
    k_h = _rope2d(_heads(k), *rope) * k_scale
    qc_h, kc_h = _heads(qc), _heads(kc) * k_scale
    lg_f = -jnp.exp(ret_decay_f.astype(jnp.float32))
    lg_b = -jnp.exp(ret_decay_b.astype(jnp.float32))
    o, oc = _retention(q_h, k_h, _heads(v), qc_h, kc_h, _heads(vc), lg_f, lg_b, with_ctx_out)
    y = _merge(ys.astype(h.dtype), _ret_out(o, g), gs, gr, w_ssm_glu, w_ret_up, w_out)
    if not with_ctx_out:
        return y, None
    yc = _merge(ysc.astype(hc.dtype), _ret_out(oc, gc), gsc, grc, w_ssm_glu, w_ret_up, w_out)
    return y, yc


def _peer(h, w_q, sub_keys, expert_u, expert_v):
    shape = h.shape
    tokens = h.reshape(-1, D_MODEL)
    n_blocks = tokens.shape[0] // PEER_BLOCK
    half = PEER_DKEY // 2

    def block(tb):
        t = tb.shape[0]
        qr = (tb @ w_q).reshape(t, PEER_HEADS, 2, half)
        s = jnp.einsum('thsk,hsnk->thsn', qr, sub_keys).astype(jnp.float32)
        top_s, top_i = lax.top_k(s, PEER_TOPK)
        cand_s = top_s[:, :, 0, :, None] + top_s[:, :, 1, None, :]
        cand_i = top_i[:, :, 0, :, None] * PEER_NKEYS + top_i[:, :, 1, None, :]
        best_s, best_j = lax.top_k(cand_s.reshape(t, PEER_HEADS, PEER_TOPK * PEER_TOPK), PEER_TOPK)
        idx = jnp.take_along_axis(cand_i.reshape(t, PEER_HEADS, PEER_TOPK * PEER_TOPK), best_j, axis=-1)
        gate = jax.nn.softmax(best_s, axis=-1)
        act = jax.nn.gelu(jnp.einsum('td,thkd->thk', tb, expert_u[idx]).astype(jnp.float32),
                          approximate=False)
        w = (gate * act).astype(tb.dtype)
        return jnp.einsum('thk,thkd->td', w, expert_v[idx])

    out = lax.map(block, tokens.reshape(n_blocks, PEER_BLOCK, D_MODEL))
    return out.reshape(shape)


def setup_inputs(seed: int = 0) -> dict:
    key = jax.random.key(seed)
    ks = iter(jax.random.split(key, 40))
    nrm = lambda shape, s: jax.random.normal(next(ks), shape, jnp.float32) * s
    G, P, I, H = SSM_GROUPS, SSM_STATE, SSM_GROUP, RET_HEADS
    lam_im0 = math.pi * jnp.arange(P, dtype=jnp.float32)
    ret0 = jnp.log(-jnp.log(1.0 - 2.0 ** (-5.0 - jnp.arange(H, dtype=jnp.float32))))
    log_dt = lambda: jax.random.uniform(next(ks), (DEPTH, G), jnp.float32,
                                        math.log(DT_MIN), math.log(DT_MAX))
    return {
        "x": nrm((BATCH, SEQ, D_MODEL), 1.0),
        "c": nrm((BATCH, D_MODEL), 1.0),
        "ctx": nrm((BATCH, CTX_LEN, D_MODEL), 1.0),
        "c_ctx": nrm((D_MODEL,), 1.0),
        "w_mod": nrm((DEPTH, D_MODEL, 6 * D_MODEL), 0.5 * D_MODEL ** -0.5),
        "b_mod": nrm((DEPTH, 6 * D_MODEL), 0.02),
        "norm1_g": 1.0 + nrm((DEPTH, D_MODEL), 0.02),
        "norm2_g": 1.0 + nrm((DEPTH, D_MODEL), 0.02),
        "w_in": nrm((DEPTH, D_MODEL, IN_COLS), D_MODEL ** -0.5),
        "ssm_B_re": nrm((DEPTH, G, P, I), (2.0 * I) ** -0.5),
        "ssm_B_im": nrm((DEPTH, G, P, I), (2.0 * I) ** -0.5),
        "ssm_C_re": nrm((DEPTH, G, I, P), 0.5),
        "ssm_C_im": nrm((DEPTH, G, I, P), 0.5),
        "ssm_D": nrm((DEPTH, SSM_WIDTH), 1.0),
        "ssm_lam_re_f": -0.5 + nrm((DEPTH, G, P), 0.01),
        "ssm_lam_im_f": lam_im0 + nrm((DEPTH, G, P), 0.01),
        "ssm_log_dt_f": log_dt(),
        "ssm_lam_re_b": -0.5 + nrm((DEPTH, G, P), 0.01),
        "ssm_lam_im_b": lam_im0 + nrm((DEPTH, G, P), 0.01),
        "ssm_log_dt_b": log_dt(),
        "w_ssm_glu": nrm((DEPTH, SSM_WIDTH, 2 * D_MODEL), SSM_WIDTH ** -0.5),
        "ret_decay_f": ret0 + nrm((DEPTH, H), 0.05),
        "ret_decay_b": ret0 + nrm((DEPTH, H), 0.05),
        "w_ret_up": nrm((DEPTH, RET_WIDTH, D_MODEL), RET_WIDTH ** -0.5),
        "w_out": nrm((DEPTH, D_MODEL, D_MODEL), D_MODEL ** -0.5),
        "peer_w_q": nrm((DEPTH, D_MODEL, PEER_HEADS * PEER_DKEY), D_MODEL ** -0.5),
        "peer_sub_keys": nrm((DEPTH, PEER_HEADS, 2, PEER_NKEYS, PEER_DKEY // 2), (PEER_DKEY // 2) ** -0.5),
        "peer_u": nrm((DEPTH, PEER_EXPERTS, D_MODEL), D_MODEL ** -0.5),
        "peer_v": nrm((DEPTH, PEER_EXPERTS, D_MODEL), 0.5),
        "final_norm_g": 1.0 + nrm((D_MODEL,), 0.02),
    }


def reference(x, c, ctx, c_ctx, w_mod, b_mod, norm1_g, norm2_g, w_in, ssm_B_re, ssm_B_im,
              ssm_C_re, ssm_C_im, ssm_D, ssm_lam_re_f, ssm_lam_im_f, ssm_log_dt_f,
              ssm_lam_re_b, ssm_lam_im_b, ssm_log_dt_b, w_ssm_glu, ret_decay_f, ret_decay_b,
              w_ret_up, w_out, peer_w_q, peer_sub_keys, peer_u, peer_v, final_norm_g):
    seq = x.shape[1]
    rows = seq // GRID_W
    row_ids = jnp.repeat(jnp.arange(rows, dtype=jnp.float32), GRID_W)
    col_ids = jnp.tile(jnp.arange(GRID_W, dtype=jnp.float32), rows)
    n_freq = RET_HEAD_DIM // 4
    freqs = ROPE_BASE ** (-jnp.arange(n_freq, dtype=jnp.float32) / n_freq)
    ang_r = row_ids[:, None] * freqs[None]
    ang_c = col_ids[:, None] * freqs[None]
    rope = (jnp.cos(ang_r), jnp.sin(ang_r), jnp.cos(ang_c), jnp.sin(ang_c))

    xc = ctx
    for i in range(DEPTH):
        last = i == DEPTH - 1
        mod = jax.nn.silu(c) @ w_mod[i] + b_mod[i]
        mod_c = jax.nn.silu(c_ctx) @ w_mod[i] + b_mod[i]
        sh1, sc1, g1, sh2, sc2, g2 = jnp.split(mod[:, None, :], 6, axis=-1)
        sh1c, sc1c, g1c, sh2c, sc2c, g2c = jnp.split(mod_c, 6, axis=-1)
        h = _modulate(_rmsnorm(x, norm1_g[i]), sh1, sc1)
        hc = _modulate(_rmsnorm(xc, norm1_g[i]), sh1c, sc1c)
        y, yc = _mixer(h, hc, rope, w_in[i], ssm_B_re[i], ssm_B_im[i], ssm_C_re[i], ssm_C_im[i],
                       ssm_D[i], ssm_lam_re_f[i], ssm_lam_im_f[i], ssm_log_dt_f[i],
                       ssm_lam_re_b[i], ssm_lam_im_b[i], ssm_log_dt_b[i], w_ssm_glu[i],
                       ret_decay_f[i], ret_decay_b[i], w_ret_up[i], w_out[i], not last)
        x = x + g1 * y
        h2 = _modulate(_rmsnorm(x, norm2_g[i]), sh2, sc2)
        x = x + g2 * _peer(h2, peer_w_q[i], peer_sub_keys[i], peer_u[i], peer_v[i])
        if not last:
            xc = xc + g1c * yc
            hc2 = _modulate(_rmsnorm(xc, norm2_g[i]), sh2c, sc2c)
            xc = xc + g2c * _peer(hc2, peer_w_q[i], peer_sub_keys[i], peer_u[i], peer_v[i])
    return _rmsnorm(x, final_norm_g)
```

```python
import math
from functools import partial

import jax
import jax.numpy as jnp
from jax import lax
from jax.experimental import pallas as pl
from jax.experimental.pallas import tpu as pltpu

D_MODEL = 1024
GRID_W = 64
EPS = 1e-6
SSM_WIDTH = 512
SSM_GROUP = 16
SSM_GROUPS = SSM_WIDTH // SSM_GROUP
SSM_STATE = 64
RET_WIDTH = 512
RET_HEADS = 4
RET_HEAD_DIM = RET_WIDTH // RET_HEADS
ROPE_BASE = 10000.0
PEER_HEADS = 8
PEER_NKEYS = 128
PEER_DKEY = 128
PEER_TOPK = 16
IN_COLS = SSM_WIDTH + 4 * RET_WIDTH + 2 * D_MODEL

LANES = 128
SUBLANES = 8
VMEM_LIMIT = 56 * 1024 * 1024
TOK_BLOCK = 256
RET_CHUNK = 128
S5_CHUNK = 64
PEER_ROUTE_BLOCK = 128
PEER_TOK_BLOCK = 256
PEER_GROUP = 8
PICKS = PEER_HEADS * PEER_TOPK
SLAB_STRIDE = LANES + SUBLANES


def _cparams(sem=None):
    return pltpu.CompilerParams(dimension_semantics=sem, vmem_limit_bytes=VMEM_LIMIT)


def _mod_body(c_ref, w_ref, b_ref, o_ref):
    cv = c_ref[...]
    s = cv * jax.nn.sigmoid(cv)
    o_ref[0] = jnp.dot(s, w_ref[0], preferred_element_type=jnp.float32,
                       precision=lax.Precision.HIGHEST) + b_ref[0]


def _modulation(cvec, w_mod, b_mod):
    depth, d, d6 = w_mod.shape
    nt = d6 // d
    return pl.pallas_call(
        _mod_body,
        grid=(depth, nt),
        in_specs=[pl.BlockSpec((SUBLANES, d), lambda l, j: (0, 0)),
                  pl.BlockSpec((1, d, d), lambda l, j: (l, 0, j)),
                  pl.BlockSpec((1, 1, d), lambda l, j: (l, 0, j))],
        out_specs=pl.BlockSpec((1, SUBLANES, d), lambda l, j: (l, 0, j)),
        out_shape=jax.ShapeDtypeStruct((depth, SUBLANES, d6), jnp.float32),
        compiler_params=_cparams(("arbitrary", "arbitrary")),
        name="adaln_mod",
    )(cvec, w_mod, b_mod.reshape(depth, 1, d6))


def _mod_row(i, blocks_per_batch, ctx_blocks):
    b = i // blocks_per_batch
    j = i - b * blocks_per_batch
    return jnp.where(j < ctx_blocks, 2, b)


def _norm_mod(x, g, shift, scale):
    y = x * lax.rsqrt(jnp.mean(x * x, axis=-1, keepdims=True) + EPS)
    return (y * g) * (1.0 + scale) + shift


def _swap_halves(t):
    lane = lax.broadcasted_iota(jnp.int32, t.shape, 1)
    first = (lane % 64) < 32
    return jnp.where(first, pltpu.roll(t, 96, axis=1), pltpu.roll(t, 32, axis=1))


def _in_body(x_ref, mod_ref, g_ref, w_ref, cos_ref, sin_ref,
             u_ref, q_ref, k_ref, v_ref, gg_ref, gs_ref, gr_ref):
    m = mod_ref[0]
    h = _norm_mod(x_ref[...], g_ref[...], m[0:1], m[1:2]).astype(jnp.bfloat16)

    def proj(lo, hi):
        return jnp.dot(h, w_ref[:, lo:hi], preferred_element_type=jnp.float32)

    o = 0
    u_ref[...] = proj(o, o + SSM_WIDTH).astype(u_ref.dtype)
    o += SSM_WIDTH
    cos = cos_ref[...]
    sin = sin_ref[...]
    k_scale = RET_HEAD_DIM ** -0.5
    for dst, scl in ((q_ref, 1.0), (k_ref, k_scale)):
        t = proj(o, o + RET_WIDTH)
        for hd in range(RET_HEADS):
            th = t[:, hd * RET_HEAD_DIM:(hd + 1) * RET_HEAD_DIM]
            r = th * cos + _swap_halves(th) * sin
            if scl != 1.0:
                r = r * scl
            dst[:, hd * RET_HEAD_DIM:(hd + 1) * RET_HEAD_DIM] = r.astype(dst.dtype)
        o += RET_WIDTH
    v_ref[...] = proj(o, o + RET_WIDTH).astype(v_ref.dtype)
    o += RET_WIDTH
    gg_ref[...] = proj(o, o + RET_WIDTH).astype(gg_ref.dtype)
    o += RET_WIDTH
    gs_ref[...] = proj(o, o + D_MODEL).astype(gs_ref.dtype)
    o += D_MODEL
    gr_ref[...] = proj(o, o + D_MODEL).astype(gr_ref.dtype)


def _in_proj(xs, modl, g1, w_in_bf, cos_t, sin_t, blocks_per_batch, ctx_blocks):
    t, d = xs.shape
    tb = TOK_BLOCK
    row = lambda i: (i, 0)
    bf = jnp.bfloat16
    outs = [jax.ShapeDtypeStruct((t, SSM_WIDTH), bf)] + [jax.ShapeDtypeStruct((t, RET_WIDTH), bf)] * 4 \
        + [jax.ShapeDtypeStruct((t, D_MODEL), bf)] * 2
    return pl.pallas_call(
        _in_body,
        grid=(t // tb,),
        in_specs=[pl.BlockSpec((tb, d), row),
                  pl.BlockSpec((1, 6, d), lambda i: (_mod_row(i, blocks_per_batch, ctx_blocks), 0, 0)),
                  pl.BlockSpec((1, d), lambda i: (0, 0)),
                  pl.BlockSpec((d, IN_COLS), lambda i: (0, 0)),
                  pl.BlockSpec((tb, RET_HEAD_DIM), row),
                  pl.BlockSpec((tb, RET_HEAD_DIM), row)],
        out_specs=[pl.BlockSpec((tb, SSM_WIDTH), row)] + [pl.BlockSpec((tb, RET_WIDTH), row)] * 4
        + [pl.BlockSpec((tb, D_MODEL), row)] * 2,
        out_shape=outs,
        compiler_params=_cparams(("arbitrary",)),
        name="in_proj",
    )(xs, modl, g1, w_in_bf, cos_t, sin_t)


def _s5_operators(B_re, B_im, C_re, C_im, d_skip, lam_re_f, lam_im_f, log_dt_f,
                  lam_re_b, lam_im_b, log_dt_b):
    f32 = jnp.float32
    hp = lax.Precision.HIGHEST
    T = S5_CHUNK
    ks = jnp.arange(T + 1, dtype=f32)[:, None, None]

    def direction(lam_re, lam_im, log_dt):
        dt = jnp.exp(log_dt.astype(f32))[:, None]
        ar, ai = lam_re.astype(f32) * dt, lam_im.astype(f32) * dt
        mag = jnp.exp(ks * ar)
        pw_re, pw_im = mag * jnp.cos(ks * ai), mag * jnp.sin(ks * ai)
        x, y = pw_re[1] - 1.0, pw_im[1]
        den = lam_re * lam_re + lam_im * lam_im
        bf_re, bf_im = (x * lam_re + y * lam_im) / den, (y * lam_re - x * lam_im) / den
        bt_re = bf_re[..., None] * B_re - bf_im[..., None] * B_im
        bt_im = bf_re[..., None] * B_im + bf_im[..., None] * B_re
        cp_re = C_re[None] * pw_re[:, :, None, :] - C_im[None] * pw_im[:, :, None, :]
        cp_im = C_re[None] * pw_im[:, :, None, :] + C_im[None] * pw_re[:, :, None, :]
        taps = (jnp.einsum('kgjp,gpi->kgij', cp_re[:T], bt_re, precision=hp)
                - jnp.einsum('kgjp,gpi->kgij', cp_im[:T], bt_im, precision=hp))
        zw_re = pw_re[:T, :, :, None] * bt_re[None] - pw_im[:T, :, :, None] * bt_im[None]
        zw_im = pw_re[:T, :, :, None] * bt_im[None] + pw_im[:T, :, :, None] * bt_re[None]
        return taps, (zw_re, zw_im), (cp_re, cp_im), (pw_re[T], pw_im[T])

    taps_f, zw_f, cp_f, a_f = direction(lam_re_f, lam_im_f, log_dt_f)
    taps_b, zw_b, cp_b, a_b = direction(lam_re_b, lam_im_b, log_dt_b)
    G, I = SSM_GROUPS, SSM_GROUP
    dd = d_skip.astype(f32).reshape(G, I)
    center = taps_f[0] + taps_b[0] + dd[:, :, None] * jnp.eye(I, dtype=f32)[None]
    full = jnp.concatenate([taps_b[:0:-1], center[None], taps_f[1:]], axis=0)
    s_i = jnp.arange(T)
    toep = full[(s_i[None, :] - s_i[:, None]) + T - 1]
    m_op = toep.transpose(2, 0, 3, 1, 4).reshape(G, T * I, T * I)

    def zcols(zw, flip):
        re, im = zw
        if flip:
            re, im = re[::-1], im[::-1]
        f = lambda a: a.transpose(1, 0, 3, 2).reshape(G, T * I, SSM_STATE)
        return [f(re), f(im)]

    v_op = jnp.concatenate(zcols(zw_f, True) + zcols(zw_b, False), axis=-1)

    def wrows(cp, idx):
        re, im = cp
        f = lambda a: a[idx].transpose(1, 3, 0, 2).reshape(G, SSM_STATE, T * I)
        return [f(re), -f(im)]

    w_op = jnp.concatenate(wrows(cp_f, jnp.arange(1, T + 1)) + wrows(cp_b, T - jnp.arange(T)), axis=1)
    mv = jnp.concatenate([m_op, v_op], axis=-1).astype(jnp.bfloat16)
    a1 = jnp.stack([a_f[0], a_f[0], a_b[0], a_b[0]]).reshape(4, G * SSM_STATE)
    a2 = jnp.stack([-a_f[1], a_f[1], -a_b[1], a_b[1]]).reshape(4, G * SSM_STATE)
    return mv, w_op.astype(jnp.bfloat16), a1, a2


def _s5_intra_body(u_ref, mv_ref, y_ref, z_ref):
    r = jnp.dot(u_ref[0], mv_ref[0], preferred_element_type=jnp.float32)
    n = y_ref.shape[-1]
    y_ref[0] = r[:, :n]
    z_ref[0] = r[:, n:]


def _s5_intra(ug, mv):
    g, m, kdim = ug.shape
    n = kdim
    nz = mv.shape[-1] - n
    return pl.pallas_call(
        _s5_intra_body,
        grid=(g,),
        in_specs=[pl.BlockSpec((1, m, kdim), lambda i: (i, 0, 0)),
                  pl.BlockSpec((1, kdim, n + nz), lambda i: (i, 0, 0))],
        out_specs=[pl.BlockSpec((1, m, n), lambda i: (i, 0, 0)),
                   pl.BlockSpec((1, m, nz), lambda i: (i, 0, 0))],
        out_shape=[jax.ShapeDtypeStruct((g, m, n), jnp.float32),
                   jax.ShapeDtypeStruct((g, m, nz), jnp.float32)],
        compiler_params=_cparams(("arbitrary",)),
        name="s5_intra",
    )(ug, mv)


def _s5_scan_body(z_ref, a1_ref, a2_ref, s_ref):
    steps = z_ref.shape[0]
    a1 = a1_ref[...]
    a2 = a2_ref[...]
    row = lax.broadcasted_iota(jnp.int32, a1.shape, 0)
    even = (row % 2) == 0

    def step(k, s):
        s_ref[k] = s
        partner = jnp.where(even, pltpu.roll(s, SUBLANES - 1, axis=0), pltpu.roll(s, 1, axis=0))
        return a1 * s + a2 * partner + z_ref[k]

    lax.fori_loop(0, steps, step, jnp.zeros(a1.shape, jnp.float32))


def _s5_scan(z8, a1, a2):
    steps, r, n = z8.shape
    cb = 512
    return pl.pallas_call(
        _s5_scan_body,
        grid=(n // cb,),
        in_specs=[pl.BlockSpec((steps, r, cb), lambda i: (0, 0, i)),
                  pl.BlockSpec((r, cb), lambda i: (0, i)),
                  pl.BlockSpec((r, cb), lambda i: (0, i))],
        out_specs=pl.BlockSpec((steps, r, cb), lambda i: (0, 0, i)),
        out_shape=jax.ShapeDtypeStruct((steps, r, n), jnp.float32),
        compiler_params=_cparams(("arbitrary",)),
        name="s5_scan",
    )(z8, a1, a2)


def _s5_out_body(y_ref, s_ref, w_ref, o_ref):
    o_ref[0] = (y_ref[0] + jnp.dot(s_ref[0], w_ref[0], preferred_element_type=jnp.float32)).astype(o_ref.dtype)


def _s5_out(yi, sg, w_op):
    g, m, n = yi.shape
    ks = sg.shape[-1]
    return pl.pallas_call(
        _s5_out_body,
        grid=(g,),
        in_specs=[pl.BlockSpec((1, m, n), lambda i: (i, 0, 0)),
                  pl.BlockSpec((1, m, ks), lambda i: (i, 0, 0)),
                  pl.BlockSpec((1, ks, n), lambda i: (i, 0, 0))],
        out_specs=pl.BlockSpec((1, m, n), lambda i: (i, 0, 0)),
        out_shape=jax.ShapeDtypeStruct((g, m, n), jnp.bfloat16),
        compiler_params=_cparams(("arbitrary",)),
        name="s5_out",
    )(yi, sg, w_op)


def _s5_mix(u, ops, nb, lb, lc):
    mv, w_op, a1, a2 = ops
    G, I, P, T = SSM_GROUPS, SSM_GROUP, SSM_STATE, S5_CHUNK
    nch = lb // T
    cch = lc // T
    ug = u.reshape(nb * nch, T, G, I).transpose(2, 0, 1, 3).reshape(G, nb * nch, T * I)
    yi, z = _s5_intra(ug, mv)
    order_b = jnp.concatenate([jnp.arange(cch - 1, -1, -1), jnp.arange(nch - 1, cch - 1, -1)])
    z6 = z.reshape(G, nb, nch, 2, 2, P)
    zf = z6[:, :, :, 0]
    zb = z6[:, :, order_b, 1]
    z8 = jnp.stack([zf, zb], axis=3)
    z8 = z8.transpose(2, 1, 3, 4, 0, 5).reshape(nch, nb * 4, G * P)
    rows = nb * 4
    pad = (-rows) % SUBLANES
    reps = (rows + pad) // 4
    if pad:
        z8 = jnp.pad(z8, ((0, 0), (0, pad), (0, 0)))
    s8 = _s5_scan(z8, jnp.tile(a1, (reps, 1)), jnp.tile(a2, (reps, 1)))[:, :rows]
    s6 = s8.reshape(nch, nb, 2, 2, G, P).transpose(4, 1, 0, 2, 3, 5)
    inv_b = jnp.argsort(order_b)
    sf = s6[:, :, :, 0]
    sb = s6[:, :, inv_b, 1]
    sg = jnp.stack([sf, sb], axis=3).reshape(G, nb * nch, 4 * P).astype(jnp.bfloat16)
    y = _s5_out(yi, sg, w_op)
    return y.reshape(G, nb * nch, T, I).transpose(1, 2, 0, 3).reshape(nb * lb, G * I)


def _ret_consts(ret_decay_f, ret_decay_b):
    f32 = jnp.float32
    C = RET_CHUNK
    lg_f = -jnp.exp(ret_decay_f.astype(f32))[:, None, None]
    lg_b = -jnp.exp(ret_decay_b.astype(f32))[:, None, None]
    pos = jnp.arange(C, dtype=f32)
    diff = pos[:, None] - pos[None, :]
    dmat = jnp.where(diff >= 0, jnp.exp(jnp.where(diff >= 0, diff, 0.0)[None] * lg_f),
                     jnp.exp(jnp.where(diff < 0, -diff, 0.0)[None] * lg_b))
    col = lambda e: jnp.broadcast_to(jnp.exp(e), (RET_HEADS, C, RET_HEAD_DIM))
    p1 = pos[None, :, None]
    qdec_f = col((p1 + 1.0) * lg_f)
    kdec_f = col((C - 1.0 - p1) * lg_f)
    qdec_b = col((C - p1) * lg_b)
    kdec_b = col(p1 * lg_b)
    cd_f = col(jnp.full_like(p1, C) * lg_f)
    cd_b = col(jnp.full_like(p1, C) * lg_b)
    return dmat, qdec_f, kdec_f, cd_f, qdec_b, kdec_b, cd_b


def _ret_state_update(s_ref, b, hd, kh, vh, kdec, cd):
    kd = (kh.astype(jnp.float32) * kdec).astype(jnp.bfloat16)
    inc = lax.dot_general(kd, vh, (((0,), (0,)), ((), ())), preferred_element_type=jnp.float32)
    s_ref[b, hd] = cd * s_ref[b, hd] + inc


def _ret_fwd_body(q_ref, k_ref, v_ref, dm_ref, qd_ref, kd_ref, cd_ref, o_ref, s_ref):
    @pl.when(pl.program_id(0) == 0)
    def _():
        s_ref[...] = jnp.zeros(s_ref.shape, s_ref.dtype)

    for b in range(q_ref.shape[0]):
        for hd in range(RET_HEADS):
            sl = slice(hd * RET_HEAD_DIM, (hd + 1) * RET_HEAD_DIM)
            qh, kh, vh = q_ref[b, :, sl], k_ref[b, :, sl], v_ref[b, :, sl]
            sc = lax.dot_general(qh, kh, (((1,), (1,)), ((), ())), preferred_element_type=jnp.float32)
            p = (sc * dm_ref[hd]).astype(jnp.bfloat16)
            o = jnp.dot(p, vh, preferred_element_type=jnp.float32)
            cross = jnp.dot(qh, s_ref[b, hd].astype(jnp.bfloat16), preferred_element_type=jnp.float32)
            o_ref[b, :, sl] = o + cross * qd_ref[hd]
            _ret_state_update(s_ref, b, hd, kh, vh, kd_ref[hd], cd_ref[hd])


def _ret_bwd_body(q_ref, k_ref, v_ref, of_ref, g_ref, qd_ref, kd_ref, cd_ref, y_ref, s_ref):
    @pl.when(pl.program_id(0) == 0)
    def _():
        s_ref[...] = jnp.zeros(s_ref.shape, s_ref.dtype)

    for b in range(q_ref.shape[0]):
        for hd in range(RET_HEADS):
            sl = slice(hd * RET_HEAD_DIM, (hd + 1) * RET_HEAD_DIM)
            qh, kh, vh = q_ref[b, :, sl], k_ref[b, :, sl], v_ref[b, :, sl]
            cross = jnp.dot(qh, s_ref[b, hd].astype(jnp.bfloat16), preferred_element_type=jnp.float32)
            o = of_ref[b, :, sl] + cross * qd_ref[hd]
            o = o * lax.rsqrt(jnp.mean(o * o, axis=-1, keepdims=True) + EPS)
            gv = g_ref[b, :, sl].astype(jnp.float32)
            y_ref[b, :, sl] = (o * (gv * jax.nn.sigmoid(gv))).astype(y_ref.dtype)
            _ret_state_update(s_ref, b, hd, kh, vh, kd_ref[hd], cd_ref[hd])


def _retention(q, k, v, g, consts, nb, lb, lc):
    dmat, qdec_f, kdec_f, cd_f, qdec_b, kdec_b, cd_b = consts
    C, W = RET_CHUNK, RET_WIDTH
    nblk, cblk = lb // C, lc // C
    r3 = lambda a: a.reshape(nb, lb, W)
    fwd_idx = lambda s: (0, s, 0)
    bwd_idx = lambda s: (0, jnp.where(s < cblk, cblk - 1 - s, nblk - 1 + cblk - s), 0)
    cst = pl.BlockSpec((RET_HEADS, C, RET_HEAD_DIM), lambda s: (0, 0, 0))
    state = pltpu.VMEM((nb, RET_HEADS, RET_HEAD_DIM, RET_HEAD_DIM), jnp.float32)
    o_f = pl.pallas_call(
        _ret_fwd_body,
        grid=(nblk,),
        in_specs=[pl.BlockSpec((nb, C, W), fwd_idx)] * 3 + [cst] * 4,
        out_specs=pl.BlockSpec((nb, C, W), fwd_idx),
        out_shape=jax.ShapeDtypeStruct((nb, lb, W), jnp.float32),
        scratch_shapes=[state],
        compiler_params=_cparams(("arbitrary",)),
        name="ret_fwd",
    )(r3(q), r3(k), r3(v), dmat, qdec_f, kdec_f, cd_f)
    y = pl.pallas_call(
        _ret_bwd_body,
        grid=(nblk,),
        in_specs=[pl.BlockSpec((nb, C, W), bwd_idx)] * 5 + [cst] * 3,
        out_specs=pl.BlockSpec((nb, C, W), bwd_idx),
        out_shape=jax.ShapeDtypeStruct((nb, lb, W), jnp.bfloat16),
        scratch_shapes=[state],
        compiler_params=_cparams(("arbitrary",)),
        name="ret_bwd",
    )(r3(q), r3(k), r3(v), o_f, r3(g), qdec_b, kdec_b, cd_b)
    return y.reshape(nb * lb, W)


def _gelu_tanh(x):
    return 0.5 * x * (1.0 + jnp.tanh(math.sqrt(2.0 / math.pi) * (x + 0.044715 * x * x * x)))


def _merge_body(x_ref, ys_ref, yr_ref, gs_ref, gr_ref, mod_ref, wg_ref, wr_ref, wo_ref, o_ref):
    f32 = jnp.float32
    a_in = _gelu_tanh(ys_ref[...].astype(f32)).astype(jnp.bfloat16)
    ab = jnp.dot(a_in, wg_ref[...], preferred_element_type=f32)
    ys = ab[:, :D_MODEL] * jax.nn.sigmoid(ab[:, D_MODEL:])
    yr = jnp.dot(yr_ref[...], wr_ref[...], preferred_element_type=f32)
    m = jax.nn.sigmoid(gs_ref[...].astype(f32)) * ys + jax.nn.sigmoid(gr_ref[...].astype(f32)) * yr
    y = jnp.dot(m.astype(jnp.bfloat16), wo_ref[...], preferred_element_type=f32)
    o_ref[...] = x_ref[...] + mod_ref[0][2:3] * y


def _merge(xs, ys, yr, gs, gr, modl, wg, wr, wo, blocks_per_batch, ctx_blocks):
    t, d = xs.shape
    tb = TOK_BLOCK
    row = lambda i: (i, 0)
    full = lambda a: pl.BlockSpec(a.shape, lambda i: (0, 0))
    return pl.pallas_call(
        _merge_body,
        grid=(t // tb,),
        in_specs=[pl.BlockSpec((tb, d), row), pl.BlockSpec((tb, SSM_WIDTH), row),
                  pl.BlockSpec((tb, RET_WIDTH), row), pl.BlockSpec((tb, d), row), pl.BlockSpec((tb, d), row),
                  pl.BlockSpec((1, 6, d), lambda i: (_mod_row(i, blocks_per_batch, ctx_blocks), 0, 0)),
                  full(wg), full(wr), full(wo)],
        out_specs=pl.BlockSpec((tb, d), row),
        out_shape=jax.ShapeDtypeStruct((t, d), jnp.float32),
        compiler_params=_cparams(("arbitrary",)),
        name="merge_out",
    )(xs, ys, yr, gs, gr, modl, wg, wr, wo)


def _top16_rows(vals, payload):
    n = vals.shape[0]
    rid = lax.broadcasted_iota(jnp.int32, vals.shape, 0)
    top_v, top_p = [], []
    for _ in range(PEER_TOPK):
        m = jnp.max(vals, axis=0, keepdims=True)
        first = jnp.min(jnp.where(vals == m, rid, n), axis=0, keepdims=True)
        hit = rid == first
        top_v.append(m)
        top_p.append(jnp.max(jnp.where(hit, payload, -1), axis=0, keepdims=True))
        vals = jnp.where(hit, -jnp.inf, vals)
    return jnp.concatenate(top_v, axis=0), jnp.concatenate(top_p, axis=0)


def _route_body(x_ref, mod_ref, g_ref, wq_ref, keys_ref, hh_ref, hl_ref, idx_ref, gate_ref):
    m = mod_ref[0]
    h2 = _norm_mod(x_ref[...], g_ref[...], m[3:4], m[4:5])
    half = D_MODEL // 2
    hh_ref[...] = h2[:, :half]
    hl_ref[...] = h2[:, half:]
    q = jnp.dot(h2.astype(jnp.bfloat16), wq_ref[...], preferred_element_type=jnp.float32)
    kid = lax.broadcasted_iota(jnp.int32, (PEER_NKEYS, x_ref.shape[0]), 0)
    idx_rows, gate_rows = [], []
    for hd in range(PEER_HEADS):
        qh = q[:, hd * PEER_DKEY:(hd + 1) * PEER_DKEY].astype(jnp.bfloat16)
        tops = []
        for s in range(2):
            st = lax.dot_general(keys_ref[hd, s], qh, (((1,), (1,)), ((), ())),
                                 preferred_element_type=jnp.float32)
            tops.append(_top16_rows(st, kid))
        (s1, i1), (s2, i2) = tops
        cand_s = jnp.concatenate([s1[a:a + 1] + s2 for a in range(PEER_TOPK)], axis=0)
        cand_e = jnp.concatenate([i1[a:a + 1] * PEER_NKEYS + i2 for a in range(PEER_TOPK)], axis=0)
        best_s, best_e = _top16_rows(cand_s, cand_e)
        ex = jnp.exp(best_s - best_s[0:1])
        gate_rows.append(ex / jnp.sum(ex, axis=0, keepdims=True))
        idx_rows.append(best_e)
    idx_ref[...] = jnp.concatenate(idx_rows, axis=0).T
    gate_ref[...] = jnp.concatenate(gate_rows, axis=0).T


def _route(xs, modl, g2, wq_bf, keys_pad, blocks_per_batch, ctx_blocks):
    t, d = xs.shape
    tb = PEER_ROUTE_BLOCK
    row = lambda i: (i, 0)
    half = d // 2
    return pl.pallas_call(
        _route_body,
        grid=(t // tb,),
        in_specs=[pl.BlockSpec((tb, d), row),
                  pl.BlockSpec((1, 6, d), lambda i: (_mod_row(i, blocks_per_batch, ctx_blocks), 0, 0)),
                  pl.BlockSpec((1, d), lambda i: (0, 0)),
                  pl.BlockSpec((d, PEER_HEADS * PEER_DKEY), lambda i: (0, 0)),
                  pl.BlockSpec(keys_pad.shape, lambda i: (0, 0, 0, 0))],
        out_specs=[pl.BlockSpec((tb, half), row), pl.BlockSpec((tb, half), row),
                   pl.BlockSpec((tb, PICKS), row), pl.BlockSpec((tb, PICKS), row)],
        out_shape=[jax.ShapeDtypeStruct((t, half), jnp.float32), jax.ShapeDtypeStruct((t, half), jnp.float32),
                   jax.ShapeDtypeStruct((t, PICKS), jnp.int32), jax.ShapeDtypeStruct((t, PICKS), jnp.float32)],
        compiler_params=_cparams(("arbitrary",)),
        name="peer_route",
    )(xs, modl, g2, wq_bf, keys_pad)


def _pack_table(tab):
    n, d = tab.shape
    bits = lax.bitcast_convert_type(tab.astype(jnp.bfloat16), jnp.uint16).astype(jnp.uint32)
    word = (bits[:, :d // 2] << 16) | bits[:, d // 2:]
    return lax.bitcast_convert_type(word, jnp.int32).reshape(n, d // 2 // LANES, LANES)


def _unpack(word):
    hi = pltpu.bitcast(word & jnp.int32(-65536), jnp.float32)
    lo = pltpu.bitcast(word << 16, jnp.float32)
    return hi, lo


def _split_bf16(a):
    hi = a.astype(jnp.bfloat16)
    lo = (a - hi.astype(jnp.float32)).astype(jnp.bfloat16)
    return hi, lo


_NT = (((1,), (1,)), ((), ()))


def _expert_act_body(idx_ref, hh_ref, hl_ref, gate_ref, tab_ref, w_ref, pb_ref):
    nrow = hh_ref.shape[1]
    ones = jnp.ones((SUBLANES, 2 * LANES), jnp.bfloat16)

    def group(gi, carry):
        rows = []
        for j in range(PEER_GROUP):
            t = gi * PEER_GROUP + j
            xh, xl = hh_ref[t], hl_ref[t]
            for p in range(PICKS):
                hi, lo = _unpack(tab_ref[idx_ref[t, p]])
                pb_ref[j, pl.ds(p, nrow, stride=SLAB_STRIDE), :] = hi * xh + lo * xl
            s = pb_ref[j, 0:PICKS, :]
            for r in range(1, nrow):
                s = s + pb_ref[j, r * SLAB_STRIDE:r * SLAB_STRIDE + PICKS, :]
            s_hi, s_lo = _split_bf16(s)
            tot = lax.dot_general(ones, jnp.concatenate([s_hi, s_lo], axis=1), _NT,
                                  preferred_element_type=jnp.float32)
            rows.append(tot[0:1])
        act = jnp.concatenate(rows, axis=0)
        gl = 0.5 * act * (1.0 + lax.erf(act * (1.0 / math.sqrt(2.0))))
        t0 = pl.multiple_of(gi * PEER_GROUP, PEER_GROUP)
        w_ref[pl.ds(t0, PEER_GROUP), :] = gate_ref[pl.ds(t0, PEER_GROUP), :] * gl
        return carry

    lax.fori_loop(0, hh_ref.shape[0] // PEER_GROUP, group, 0)


def _expert_out_body(idx_ref, w_ref, tab_ref, oh_ref, ol_ref, fh_ref, fl_ref):
    nrow = oh_ref.shape[1]
    rid = lax.broadcasted_iota(jnp.int32, (SUBLANES, PICKS), 0)

    def group(gi, carry):
        for j in range(PEER_GROUP):
            t = gi * PEER_GROUP + j
            for p in range(PICKS):
                hi, lo = _unpack(tab_ref[idx_ref[t, p]])
                fh_ref[j, pl.ds(p, nrow, stride=SLAB_STRIDE), :] = hi
                fl_ref[j, pl.ds(p, nrow, stride=SLAB_STRIDE), :] = lo
            wrow = jnp.broadcast_to(w_ref[pl.ds(t, 1), :], rid.shape)
            w_hi = wrow.astype(jnp.bfloat16).astype(jnp.float32)
            lhs = jnp.where(rid == 0, w_hi, jnp.where(rid == 1, wrow - w_hi, 0.0)).astype(jnp.bfloat16)
            for buf, dst in ((fh_ref, oh_ref), (fl_ref, ol_ref)):
                outs = []
                for r in range(nrow):
                    slab = buf[j, r * SLAB_STRIDE:r * SLAB_STRIDE + PICKS, :].astype(jnp.bfloat16)
                    res = jnp.dot(lhs, slab, preferred_element_type=jnp.float32)
                    outs.append(res[0:1] + res[1:2])
                dst[t] = jnp.concatenate(outs, axis=0)
        return carry

    lax.fori_loop(0, oh_ref.shape[0] // PEER_GROUP, group, 0)


def _expert_specs(t):
    nt = PEER_TOK_BLOCK
    nrow = D_MODEL // 2 // LANES
    smem = pl.BlockSpec((nt, PICKS), lambda i: (i, 0), memory_space=pltpu.SMEM)
    vrow = pl.BlockSpec((nt, PICKS), lambda i: (i, 0))
    half = pl.BlockSpec((nt, nrow, LANES), lambda i: (i, 0, 0))
    table = pl.BlockSpec(memory_space=pltpu.VMEM)
    stage = pltpu.VMEM((PEER_GROUP, nrow * SLAB_STRIDE, LANES), jnp.float32)
    return nt, nrow, smem, vrow, half, table, stage


def _expert_act(idx, hh, hl, gate, tab_u):
    t = idx.shape[0]
    nt, nrow, smem, vrow, half, table, stage = _expert_specs(t)
    return pl.pallas_call(
        _expert_act_body,
        grid=(t // nt,),
        in_specs=[smem, half, half, vrow, table],
        out_specs=vrow,
        out_shape=jax.ShapeDtypeStruct((t, PICKS), jnp.float32),
        scratch_shapes=[stage],
        compiler_params=_cparams(("arbitrary",)),
        name="peer_act",
    )(idx, hh.reshape(t, nrow, LANES), hl.reshape(t, nrow, LANES), gate, tab_u)


def _expert_out(idx, w, tab_v):
    t = idx.shape[0]
    nt, nrow, smem, vrow, half, table, stage = _expert_specs(t)
    oh, ol = pl.pallas_call(
        _expert_out_body,
        grid=(t // nt,),
        in_specs=[smem, vrow, table],
        out_specs=[half, half],
        out_shape=[jax.ShapeDtypeStruct((t, nrow, LANES), jnp.float32)] * 2,
        scratch_shapes=[stage, stage],
        compiler_params=_cparams(("arbitrary",)),
        name="peer_out",
    )(idx, w, tab_v)
    return oh.reshape(t, nrow * LANES), ol.reshape(t, nrow * LANES)


def _residual_body(x_ref, ph_ref, pl_ref, mod_ref, o_ref):
    gate = mod_ref[0][5:6]
    half = x_ref.shape[1] // 2
    o_ref[:, :half] = x_ref[:, :half] + gate[:, :half] * ph_ref[...]
    o_ref[:, half:] = x_ref[:, half:] + gate[:, half:] * pl_ref[...]


def _residual(xs, ph, plo, modl, blocks_per_batch, ctx_blocks):
    t, d = xs.shape
    tb = TOK_BLOCK
    row = lambda i: (i, 0)
    return pl.pallas_call(
        _residual_body,
        grid=(t // tb,),
        in_specs=[pl.BlockSpec((tb, d), row), pl.BlockSpec((tb, d // 2), row), pl.BlockSpec((tb, d // 2), row),
                  pl.BlockSpec((1, 6, d), lambda i: (_mod_row(i, blocks_per_batch, ctx_blocks), 0, 0))],
        out_specs=pl.BlockSpec((tb, d), row),
        out_shape=jax.ShapeDtypeStruct((t, d), jnp.float32),
        compiler_params=_cparams(("arbitrary",)),
        name="peer_residual",
    )(xs, ph, plo, modl)


def _final_norm_body(x_ref, g_ref, o_ref):
    xf = x_ref[...]
    o_ref[...] = xf * lax.rsqrt(jnp.mean(xf * xf, axis=-1, keepdims=True) + EPS) * g_ref[...]


def _final_norm(x2, g):
    rows, d = x2.shape
    tm = TOK_BLOCK
    return pl.pallas_call(
        _final_norm_body,
        grid=(rows // tm,),
        in_specs=[pl.BlockSpec((tm, d), lambda i: (i, 0)), pl.BlockSpec((1, d), lambda i: (0, 0))],
        out_specs=pl.BlockSpec((tm, d), lambda i: (i, 0)),
        out_shape=jax.ShapeDtypeStruct((rows, d), x2.dtype),
        compiler_params=_cparams(("arbitrary",)),
        name="final_norm",
    )(x2, g.reshape(1, d))


def _rope_tables(seq, lc, nb):
    rows = seq // GRID_W
    row_ids = jnp.repeat(jnp.arange(rows, dtype=jnp.float32), GRID_W)
    col_ids = jnp.tile(jnp.arange(GRID_W, dtype=jnp.float32), rows)
    n_freq = RET_HEAD_DIM // 4
    freqs = ROPE_BASE ** (-jnp.arange(n_freq, dtype=jnp.float32) / n_freq)
    ang_r, ang_c = row_ids[:, None] * freqs[None], col_ids[:, None] * freqs[None]
    cos = jnp.concatenate([jnp.cos(ang_r), jnp.cos(ang_r), jnp.cos(ang_c), jnp.cos(ang_c)], axis=-1)
    sin = jnp.concatenate([-jnp.sin(ang_r), jnp.sin(ang_r), -jnp.sin(ang_c), jnp.sin(ang_c)], axis=-1)
    cos = jnp.concatenate([jnp.ones((lc, RET_HEAD_DIM), jnp.float32), cos], axis=0)
    sin = jnp.concatenate([jnp.zeros((lc, RET_HEAD_DIM), jnp.float32), sin], axis=0)
    return jnp.tile(cos, (nb, 1)), jnp.tile(sin, (nb, 1))


def kernel(x, c, ctx, c_ctx, w_mod, b_mod, norm1_g, norm2_g, w_in, ssm_B_re, ssm_B_im,
           ssm_C_re, ssm_C_im, ssm_D, ssm_lam_re_f, ssm_lam_im_f, ssm_log_dt_f,
           ssm_lam_re_b, ssm_lam_im_b, ssm_log_dt_b, w_ssm_glu, ret_decay_f, ret_decay_b,
           w_ret_up, w_out, peer_w_q, peer_sub_keys, peer_u, peer_v, final_norm_g):
    nb, seq, d = x.shape
    lc = ctx.shape[1]
    depth = w_mod.shape[0]
    lb = lc + seq
    bf = jnp.bfloat16
    assert nb == 2 and d == D_MODEL and lc % TOK_BLOCK == 0 and seq % TOK_BLOCK == 0
    bpb, cbl = lb // TOK_BLOCK, lc // TOK_BLOCK
    bpb_r, cbl_r = lb // PEER_ROUTE_BLOCK, lc // PEER_ROUTE_BLOCK

    xs = jnp.concatenate([ctx, x], axis=1).reshape(nb * lb, d)
    cvec = jnp.zeros((SUBLANES, d), jnp.float32).at[:nb].set(c).at[nb].set(c_ctx)
    mod_all = _modulation(cvec, w_mod, b_mod)[:, :nb + 1].reshape(depth, nb + 1, 6, d)
    cos_t, sin_t = _rope_tables(seq, lc, nb)
    half = PEER_DKEY // 2

    for i in range(depth):
        modl = mod_all[i]
        s5_ops = _s5_operators(ssm_B_re[i], ssm_B_im[i], ssm_C_re[i], ssm_C_im[i], ssm_D[i],
                               ssm_lam_re_f[i], ssm_lam_im_f[i], ssm_log_dt_f[i],
                               ssm_lam_re_b[i], ssm_lam_im_b[i], ssm_log_dt_b[i])
        ret_c = _ret_consts(ret_decay_f[i], ret_decay_b[i])
        u, q, k, v, g, gs, gr = _in_proj(xs, modl, norm1_g[i].reshape(1, d), w_in[i].astype(bf),
                                         cos_t, sin_t, bpb, cbl)
        ys = _s5_mix(u, s5_ops, nb, lb, lc)
        yr = _retention(q, k, v, g, ret_c, nb, lb, lc)
        xs = _merge(xs, ys, yr, gs, gr, modl, w_ssm_glu[i].astype(bf), w_ret_up[i].astype(bf),
                    w_out[i].astype(bf), bpb, cbl)
        sk = peer_sub_keys[i].astype(bf)
        zeros = jnp.zeros_like(sk)
        keys_pad = jnp.stack([jnp.concatenate([sk[:, 0], zeros[:, 0]], axis=-1),
                              jnp.concatenate([zeros[:, 1], sk[:, 1]], axis=-1)], axis=1)
        hh, hl, idx, gate = _route(xs, modl, norm2_g[i].reshape(1, d), peer_w_q[i].astype(bf),
                                   keys_pad, bpb_r, cbl_r)
        wts = _expert_act(idx, hh, hl, gate, _pack_table(peer_u[i]))
        ph, plo = _expert_out(idx, wts, _pack_table(peer_v[i]))
        xs = _residual(xs, ph, plo, modl, bpb, cbl)

    lat = xs.reshape(nb, lb, d)[:, lc:].reshape(nb * seq, d)
    return _final_norm(lat, final_norm_g).reshape(nb, seq, d)
```

```python
import math
from functools import partial

import jax
import jax.numpy as jnp
from jax import lax
from jax.experimental import pallas as pl
from jax.experimental.pallas import tpu as pltpu

D_MODEL = 1024
GRID_W = 64
EPS = 1e-6
SSM_WIDTH = 512
SSM_GROUP = 16
SSM_GROUPS = SSM_WIDTH // SSM_GROUP
SSM_STATE = 64
RET_WIDTH = 512
RET_HEADS = 4
RET_HEAD_DIM = RET_WIDTH // RET_HEADS
ROPE_BASE = 10000.0
PEER_HEADS = 8
PEER_NKEYS = 128
PEER_DKEY = 128
PEER_TOPK = 16
IN_COLS = SSM_WIDTH + 4 * RET_WIDTH + 2 * D_MODEL

LANES = 128
SUBLANES = 8
VMEM_LIMIT = 56 * 1024 * 1024
TOK_BLOCK = 256
RET_CHUNK = 128
S5_CHUNK = 64
PEER_ROUTE_BLOCK = 128
PEER_TOK_BLOCK = 256
PEER_GROUP = 8
PICKS = PEER_HEADS * PEER_TOPK
SLAB_STRIDE = LANES + SUBLANES


def _cparams(sem=None):
    return pltpu.CompilerParams(dimension_semantics=sem, vmem_limit_bytes=VMEM_LIMIT)


def _mod_body(c_ref, w_ref, b_ref, o_ref):
    cv = c_ref[...]
    s = cv * jax.nn.sigmoid(cv)
    o_ref[0] = jnp.dot(s, w_ref[0], preferred_element_type=jnp.float32,
                       precision=lax.Precision.HIGHEST) + b_ref[0]


def _modulation(cvec, w_mod, b_mod):
    depth, d, d6 = w_mod.shape
    nt = d6 // d
    return pl.pallas_call(
        _mod_body,
        grid=(depth, nt),
        in_specs=[pl.BlockSpec((SUBLANES, d), lambda l, j: (0, 0)),
                  pl.BlockSpec((1, d, d), lambda l, j: (l, 0, j)),
                  pl.BlockSpec((1, 1, d), lambda l, j: (l, 0, j))],
        out_specs=pl.BlockSpec((1, SUBLANES, d), lambda l, j: (l, 0, j)),
        out_shape=jax.ShapeDtypeStruct((depth, SUBLANES, d6), jnp.float32),
        compiler_params=_cparams(("arbitrary", "arbitrary")),
        name="adaln_mod",
    )(cvec, w_mod, b_mod.reshape(depth, 1, d6))


def _mod_row(i, blocks_per_batch, ctx_blocks):
    b = i // blocks_per_batch
    j = i - b * blocks_per_batch
    return jnp.where(j < ctx_blocks, 2, b)


def _norm_mod(x, g, shift, scale):
    y = x * lax.rsqrt(jnp.mean(x * x, axis=-1, keepdims=True) + EPS)
    return (y * g) * (1.0 + scale) + shift


def _swap_halves(t):
    lane = lax.broadcasted_iota(jnp.int32, t.shape, 1)
    first = (lane % 64) < 32
    return jnp.where(first, pltpu.roll(t, 96, axis=1), pltpu.roll(t, 32, axis=1))


def _in_body(x_ref, mod_ref, g_ref, w_ref, cos_ref, sin_ref,
             u_ref, q_ref, k_ref, v_ref, gg_ref, gs_ref, gr_ref):
    m = mod_ref[0]
    h = _norm_mod(x_ref[...], g_ref[...], m[0:1], m[1:2]).astype(jnp.bfloat16)

    def proj(lo, hi):
        return jnp.dot(h, w_ref[:, lo:hi], preferred_element_type=jnp.float32)

    o = 0
    u_ref[...] = proj(o, o + SSM_WIDTH).astype(u_ref.dtype)
    o += SSM_WIDTH
    cos = cos_ref[...]
    sin = sin_ref[...]
    k_scale = RET_HEAD_DIM ** -0.5
    for dst, scl in ((q_ref, 1.0), (k_ref, k_scale)):
        t = proj(o, o + RET_WIDTH)
        for hd in range(RET_HEADS):
            th = t[:, hd * RET_HEAD_DIM:(hd + 1) * RET_HEAD_DIM]
            r = th * cos + _swap_halves(th) * sin
            if scl != 1.0:
                r = r * scl
            dst[:, hd * RET_HEAD_DIM:(hd + 1) * RET_HEAD_DIM] = r.astype(dst.dtype)
        o += RET_WIDTH
    v_ref[...] = proj(o, o + RET_WIDTH).astype(v_ref.dtype)
    o += RET_WIDTH
    gg_ref[...] = proj(o, o + RET_WIDTH).astype(gg_ref.dtype)
    o += RET_WIDTH
    gs_ref[...] = proj(o, o + D_MODEL).astype(gs_ref.dtype)
    o += D_MODEL
    gr_ref[...] = proj(o, o + D_MODEL).astype(gr_ref.dtype)


def _in_proj(xs, modl, g1, w_in_bf, cos_t, sin_t, blocks_per_batch, ctx_blocks):
    t, d = xs.shape
    tb = TOK_BLOCK
    row = lambda i: (i, 0)
    bf = jnp.bfloat16
    outs = [jax.ShapeDtypeStruct((t, SSM_WIDTH), bf)] + [jax.ShapeDtypeStruct((t, RET_WIDTH), bf)] * 4 \
        + [jax.ShapeDtypeStruct((t, D_MODEL), bf)] * 2
    return pl.pallas_call(
        _in_body,
        grid=(t // tb,),
        in_specs=[pl.BlockSpec((tb, d), row),
                  pl.BlockSpec((1, 6, d), lambda i: (_mod_row(i, blocks_per_batch, ctx_blocks), 0, 0)),
                  pl.BlockSpec((1, d), lambda i: (0, 0)),
                  pl.BlockSpec((d, IN_COLS), lambda i: (0, 0)),
                  pl.BlockSpec((tb, RET_HEAD_DIM), row),
                  pl.BlockSpec((tb, RET_HEAD_DIM), row)],
        out_specs=[pl.BlockSpec((tb, SSM_WIDTH), row)] + [pl.BlockSpec((tb, RET_WIDTH), row)] * 4
        + [pl.BlockSpec((tb, D_MODEL), row)] * 2,
        out_shape=outs,
        compiler_params=_cparams(("arbitrary",)),
        name="in_proj",
    )(xs, modl, g1, w_in_bf, cos_t, sin_t)


def _s5_operators(B_re, B_im, C_re, C_im, d_skip, lam_re_f, lam_im_f, log_dt_f,
                  lam_re_b, lam_im_b, log_dt_b):
    f32 = jnp.float32
    hp = lax.Precision.HIGHEST
    T = S5_CHUNK
    ks = jnp.arange(T + 1, dtype=f32)[:, None, None]

    def direction(lam_re, lam_im, log_dt):
        dt = jnp.exp(log_dt.astype(f32))[:, None]
        ar, ai = lam_re.astype(f32) * dt, lam_im.astype(f32) * dt
        mag = jnp.exp(ks * ar)
        pw_re, pw_im = mag * jnp.cos(ks * ai), mag * jnp.sin(ks * ai)
        x, y = pw_re[1] - 1.0, pw_im[1]
        den = lam_re * lam_re + lam_im * lam_im
        bf_re, bf_im = (x * lam_re + y * lam_im) / den, (y * lam_re - x * lam_im) / den
        bt_re = bf_re[..., None] * B_re - bf_im[..., None] * B_im
        bt_im = bf_re[..., None] * B_im + bf_im[..., None] * B_re
        cp_re = C_re[None] * pw_re[:, :, None, :] - C_im[None] * pw_im[:, :, None, :]
        cp_im = C_re[None] * pw_im[:, :, None, :] + C_im[None] * pw_re[:, :, None, :]
        taps = (jnp.einsum('kgjp,gpi->kgij', cp_re[:T], bt_re, precision=hp)
                - jnp.einsum('kgjp,gpi->kgij', cp_im[:T], bt_im, precision=hp))
        zw_re = pw_re[:T, :, :, None] * bt_re[None] - pw_im[:T, :, :, None] * bt_im[None]
        zw_im = pw_re[:T, :, :, None] * bt_im[None] + pw_im[:T, :, :, None] * bt_re[None]
        return taps, (zw_re, zw_im), (cp_re, cp_im), (pw_re[T], pw_im[T])

    taps_f, zw_f, cp_f, a_f = direction(lam_re_f, lam_im_f, log_dt_f)
    taps_b, zw_b, cp_b, a_b = direction(lam_re_b, lam_im_b, log_dt_b)
    G, I = SSM_GROUPS, SSM_GROUP
    dd = d_skip.astype(f32).reshape(G, I)
    center = taps_f[0] + taps_b[0] + dd[:, :, None] * jnp.eye(I, dtype=f32)[None]
    full = jnp.concatenate([taps_b[:0:-1], center[None], taps_f[1:]], axis=0)
    s_i = jnp.arange(T)
    toep = full[(s_i[None, :] - s_i[:, None]) + T - 1]
    m_op = toep.transpose(2, 0, 3, 1, 4).reshape(G, T * I, T * I)

    def zcols(zw, flip):
        re, im = zw
        if flip:
            re, im = re[::-1], im[::-1]
        f = lambda a: a.transpose(1, 0, 3, 2).reshape(G, T * I, SSM_STATE)
        return [f(re), f(im)]

    v_op = jnp.concatenate(zcols(zw_f, True) + zcols(zw_b, False), axis=-1)

    def wrows(cp, idx):
        re, im = cp
        f = lambda a: a[idx].transpose(1, 3, 0, 2).reshape(G, SSM_STATE, T * I)
        return [f(re), -f(im)]

    w_op = jnp.concatenate(wrows(cp_f, jnp.arange(1, T + 1)) + wrows(cp_b, T - jnp.arange(T)), axis=1)
    mv = jnp.concatenate([m_op, v_op], axis=-1).astype(jnp.bfloat16)
    a1 = jnp.stack([a_f[0], a_f[0], a_b[0], a_b[0]]).reshape(4, G * SSM_STATE)
    a2 = jnp.stack([-a_f[1], a_f[1], -a_b[1], a_b[1]]).reshape(4, G * SSM_STATE)
    return mv, w_op.astype(jnp.bfloat16), a1, a2


def _s5_intra_body(u_ref, mv_ref, y_ref, z_ref):
    r = jnp.dot(u_ref[0], mv_ref[0], preferred_element_type=jnp.float32)
    n = y_ref.shape[-1]
    y_ref[0] = r[:, :n]
    z_ref[0] = r[:, n:]


def _s5_intra(ug, mv):
    g, m, kdim = ug.shape
    n = kdim
    nz = mv.shape[-1] - n
    return pl.pallas_call(
        _s5_intra_body,
        grid=(g,),
        in_specs=[pl.BlockSpec((1, m, kdim), lambda i: (i, 0, 0)),
                  pl.BlockSpec((1, kdim, n + nz), lambda i: (i, 0, 0))],
        out_specs=[pl.BlockSpec((1, m, n), lambda i: (i, 0, 0)),
                   pl.BlockSpec((1, m, nz), lambda i: (i, 0, 0))],
        out_shape=[jax.ShapeDtypeStruct((g, m, n), jnp.float32),
                   jax.ShapeDtypeStruct((g, m, nz), jnp.float32)],
        compiler_params=_cparams(("arbitrary",)),
        name="s5_intra",
    )(ug, mv)


def _s5_scan_body(z_ref, a1_ref, a2_ref, s_ref):
    steps = z_ref.shape[0]
    a1 = a1_ref[...]
    a2 = a2_ref[...]
    row = lax.broadcasted_iota(jnp.int32, a1.shape, 0)
    even = (row % 2) == 0

    def step(k, s):
        s_ref[k] = s
        partner = jnp.where(even, pltpu.roll(s, SUBLANES - 1, axis=0), pltpu.roll(s, 1, axis=0))
        return a1 * s + a2 * partner + z_ref[k]

    lax.fori_loop(0, steps, step, jnp.zeros(a1.shape, jnp.float32))


def _s5_scan(z8, a1, a2):
    steps, r, n = z8.shape
    cb = 512
    return pl.pallas_call(
        _s5_scan_body,
        grid=(n // cb,),
        in_specs=[pl.BlockSpec((steps, r, cb), lambda i: (0, 0, i)),
                  pl.BlockSpec((r, cb), lambda i: (0, i)),
                  pl.BlockSpec((r, cb), lambda i: (0, i))],
        out_specs=pl.BlockSpec((steps, r, cb), lambda i: (0, 0, i)),
        out_shape=jax.ShapeDtypeStruct((steps, r, n), jnp.float32),
        compiler_params=_cparams(("arbitrary",)),
        name="s5_scan",
    )(z8, a1, a2)


def _s5_out_body(y_ref, s_ref, w_ref, o_ref):
    o_ref[0] = (y_ref[0] + jnp.dot(s_ref[0], w_ref[0], preferred_element_type=jnp.float32)).astype(o_ref.dtype)


def _s5_out(yi, sg, w_op):
    g, m, n = yi.shape
    ks = sg.shape[-1]
    return pl.pallas_call(
        _s5_out_body,
        grid=(g,),
        in_specs=[pl.BlockSpec((1, m, n), lambda i: (i, 0, 0)),
                  pl.BlockSpec((1, m, ks), lambda i: (i, 0, 0)),
                  pl.BlockSpec((1, ks, n), lambda i: (i, 0, 0))],
        out_specs=pl.BlockSpec((1, m, n), lambda i: (i, 0, 0)),
        out_shape=jax.ShapeDtypeStruct((g, m, n), jnp.bfloat16),
        compiler_params=_cparams(("arbitrary",)),
        name="s5_out",
    )(yi, sg, w_op)


def _s5_mix(u, ops, nb, lb, lc):
    mv, w_op, a1, a2 = ops
    G, I, P, T = SSM_GROUPS, SSM_GROUP, SSM_STATE, S5_CHUNK
    nch = lb // T
    cch = lc // T
    ug = u.reshape(nb * nch, T, G, I).transpose(2, 0, 1, 3).reshape(G, nb * nch, T * I)
    yi, z = _s5_intra(ug, mv)
    order_b = jnp.concatenate([jnp.arange(cch - 1, -1, -1), jnp.arange(nch - 1, cch - 1, -1)])
    z6 = z.reshape(G, nb, nch, 2, 2, P)
    zf = z6[:, :, :, 0]
    zb = z6[:, :, order_b, 1]
    z8 = jnp.stack([zf, zb], axis=3)
    z8 = z8.transpose(2, 1, 3, 4, 0, 5).reshape(nch, nb * 4, G * P)
    rows = nb * 4
    pad = (-rows) % SUBLANES
    reps = (rows + pad) // 4
    if pad:
        z8 = jnp.pad(z8, ((0, 0), (0, pad), (0, 0)))
    s8 = _s5_scan(z8, jnp.tile(a1, (reps, 1)), jnp.tile(a2, (reps, 1)))[:, :rows]
    s6 = s8.reshape(nch, nb, 2, 2, G, P).transpose(4, 1, 0, 2, 3, 5)
    inv_b = jnp.argsort(order_b)
    sf = s6[:, :, :, 0]
    sb = s6[:, :, inv_b, 1]
    sg = jnp.stack([sf, sb], axis=3).reshape(G, nb * nch, 4 * P).astype(jnp.bfloat16)
    y = _s5_out(yi, sg, w_op)
    return y.reshape(G, nb * nch, T, I).transpose(1, 2, 0, 3).reshape(nb * lb, G * I)


def _ret_consts(ret_decay_f, ret_decay_b):
    f32 = jnp.float32
    C = RET_CHUNK
    lg_f = -jnp.exp(ret_decay_f.astype(f32))[:, None, None]
    lg_b = -jnp.exp(ret_decay_b.astype(f32))[:, None, None]
    pos = jnp.arange(C, dtype=f32)
    diff = pos[:, None] - pos[None, :]
    dmat = jnp.where(diff >= 0, jnp.exp(jnp.where(diff >= 0, diff, 0.0)[None] * lg_f),
                     jnp.exp(jnp.where(diff < 0, -diff, 0.0)[None] * lg_b))
    col = lambda e: jnp.broadcast_to(jnp.exp(e), (RET_HEADS, C, RET_HEAD_DIM))
    p1 = pos[None, :, None]
    qdec_f = col((p1 + 1.0) * lg_f)
    kdec_f = col((C - 1.0 - p1) * lg_f)
    qdec_b = col((C - p1) * lg_b)
    kdec_b = col(p1 * lg_b)
    cd_f = col(jnp.full_like(p1, C) * lg_f)
    cd_b = col(jnp.full_like(p1, C) * lg_b)
    return dmat, qdec_f, kdec_f, cd_f, qdec_b, kdec_b, cd_b


def _ret_state_update(s_ref, b, hd, kh, vh, kdec, cd):
    kd = (kh.astype(jnp.float32) * kdec).astype(jnp.bfloat16)
    inc = lax.dot_general(kd, vh, (((0,), (0,)), ((), ())), preferred_element_type=jnp.float32)
    s_ref[b, hd] = cd * s_ref[b, hd] + inc


def _ret_fwd_body(q_ref, k_ref, v_ref, dm_ref, qd_ref, kd_ref, cd_ref, o_ref, s_ref):
    @pl.when(pl.program_id(0) == 0)
    def _():
        s_ref[...] = jnp.zeros(s_ref.shape, s_ref.dtype)

    for b in range(q_ref.shape[0]):
        for hd in range(RET_HEADS):
            sl = slice(hd * RET_HEAD_DIM, (hd + 1) * RET_HEAD_DIM)
            qh, kh, vh = q_ref[b, :, sl], k_ref[b, :, sl], v_ref[b, :, sl]
            sc = lax.dot_general(qh, kh, (((1,), (1,)), ((), ())), preferred_element_type=jnp.float32)
            p = (sc * dm_ref[hd]).astype(jnp.bfloat16)
            o = jnp.dot(p, vh, preferred_element_type=jnp.float32)
            cross = jnp.dot(qh, s_ref[b, hd].astype(jnp.bfloat16), preferred_element_type=jnp.float32)
            o_ref[b, :, sl] = o + cross * qd_ref[hd]
            _ret_state_update(s_ref, b, hd, kh, vh, kd_ref[hd], cd_ref[hd])


def _ret_bwd_body(q_ref, k_ref, v_ref, of_ref, g_ref, qd_ref, kd_ref, cd_ref, y_ref, s_ref):
    @pl.when(pl.program_id(0) == 0)
    def _():
        s_ref[...] = jnp.zeros(s_ref.shape, s_ref.dtype)

    for b in range(q_ref.shape[0]):
        for hd in range(RET_HEADS):
            sl = slice(hd * RET_HEAD_DIM, (hd + 1) * RET_HEAD_DIM)
            qh, kh, vh = q_ref[b, :, sl], k_ref[b, :, sl], v_ref[b, :, sl]
            cross = jnp.dot(qh, s_ref[b, hd].astype(jnp.bfloat16), preferred_element_type=jnp.float32)
            o = of_ref[b, :, sl] + cross * qd_ref[hd]
            o = o * lax.rsqrt(jnp.mean(o * o, axis=-1, keepdims=True) + EPS)
            gv = g_ref[b, :, sl].astype(jnp.float32)
            y_ref[b, :, sl] = (o * (gv * jax.nn.sigmoid(gv))).astype(y_ref.dtype)
            _ret_state_update(s_ref, b, hd, kh, vh, kd_ref[hd], cd_ref[hd])


def _retention(q, k, v, g, consts, nb, lb, lc):
    dmat, qdec_f, kdec_f, cd_f, qdec_b, kdec_b, cd_b = consts
    C, W = RET_CHUNK, RET_WIDTH
    nblk, cblk = lb // C, lc // C
    r3 = lambda a: a.reshape(nb, lb, W)
    fwd_idx = lambda s: (0, s, 0)
    bwd_idx = lambda s: (0, jnp.where(s < cblk, cblk - 1 - s, nblk - 1 + cblk - s), 0)
    cst = pl.BlockSpec((RET_HEADS, C, RET_HEAD_DIM), lambda s: (0, 0, 0))
    state = pltpu.VMEM((nb, RET_HEADS, RET_HEAD_DIM, RET_HEAD_DIM), jnp.float32)
    o_f = pl.pallas_call(
        _ret_fwd_body,
        grid=(nblk,),
        in_specs=[pl.BlockSpec((nb, C, W), fwd_idx)] * 3 + [cst] * 4,
        out_specs=pl.BlockSpec((nb, C, W), fwd_idx),
        out_shape=jax.ShapeDtypeStruct((nb, lb, W), jnp.float32),
        scratch_shapes=[state],
        compiler_params=_cparams(("arbitrary",)),
        name="ret_fwd",
    )(r3(q), r3(k), r3(v), dmat, qdec_f, kdec_f, cd_f)
    y = pl.pallas_call(
        _ret_bwd_body,
        grid=(nblk,),
        in_specs=[pl.BlockSpec((nb, C, W), bwd_idx)] * 5 + [cst] * 3,
        out_specs=pl.BlockSpec((nb, C, W), bwd_idx),
        out_shape=jax.ShapeDtypeStruct((nb, lb, W), jnp.bfloat16),
        scratch_shapes=[state],
        compiler_params=_cparams(("arbitrary",)),
        name="ret_bwd",
    )(r3(q), r3(k), r3(v), o_f, r3(g), qdec_b, kdec_b, cd_b)
    return y.reshape(nb * lb, W)


def _gelu_tanh(x):
    return 0.5 * x * (1.0 + jnp.tanh(math.sqrt(2.0 / math.pi) * (x + 0.044715 * x * x * x)))


def _merge_body(x_ref, ys_ref, yr_ref, gs_ref, gr_ref, mod_ref, wg_ref, wr_ref, wo_ref, o_ref):
    f32 = jnp.float32
    a_in = _gelu_tanh(ys_ref[...].astype(f32)).astype(jnp.bfloat16)
    ab = jnp.dot(a_in, wg_ref[...], preferred_element_type=f32)
    ys = ab[:, :D_MODEL] * jax.nn.sigmoid(ab[:, D_MODEL:])
    yr = jnp.dot(yr_ref[...], wr_ref[...], preferred_element_type=f32)
    m = jax.nn.sigmoid(gs_ref[...].astype(f32)) * ys + jax.nn.sigmoid(gr_ref[...].astype(f32)) * yr
    y = jnp.dot(m.astype(jnp.bfloat16), wo_ref[...], preferred_element_type=f32)
    o_ref[...] = x_ref[...] + mod_ref[0][2:3] * y


def _merge(xs, ys, yr, gs, gr, modl, wg, wr, wo, blocks_per_batch, ctx_blocks):
    t, d = xs.shape
    tb = TOK_BLOCK
    row = lambda i: (i, 0)
    full = lambda a: pl.BlockSpec(a.shape, lambda i: (0, 0))
    return pl.pallas_call(
        _merge_body,
        grid=(t // tb,),
        in_specs=[pl.BlockSpec((tb, d), row), pl.BlockSpec((tb, SSM_WIDTH), row),
                  pl.BlockSpec((tb, RET_WIDTH), row), pl.BlockSpec((tb, d), row), pl.BlockSpec((tb, d), row),
                  pl.BlockSpec((1, 6, d), lambda i: (_mod_row(i, blocks_per_batch, ctx_blocks), 0, 0)),
                  full(wg), full(wr), full(wo)],
        out_specs=pl.BlockSpec((tb, d), row),
        out_shape=jax.ShapeDtypeStruct((t, d), jnp.float32),
        compiler_params=_cparams(("arbitrary",)),
        name="merge_out",
    )(xs, ys, yr, gs, gr, modl, wg, wr, wo)


def _top16_rows(vals, payload=None):
    n = vals.shape[0]
    rid = lax.broadcasted_iota(jnp.int32, vals.shape, 0).astype(jnp.float32)
    top_v, top_p = [], []
    for _ in range(PEER_TOPK):
        m = jnp.max(vals, axis=0, keepdims=True)
        first = jnp.min(jnp.where(vals == m, rid, float(n)), axis=0, keepdims=True)
        hit = rid == first
        top_v.append(m)
        top_p.append(first if payload is None else
                     jnp.max(jnp.where(hit, payload, -1.0), axis=0, keepdims=True))
        vals = jnp.where(hit, -jnp.inf, vals)
    return jnp.concatenate(top_v, axis=0), jnp.concatenate(top_p, axis=0)


def _pair_candidates(s1, i1, s2, i2):
    k = PEER_TOPK
    bid = lax.broadcasted_iota(jnp.int32, (SUBLANES, s1.shape[1]), 0)
    cs, ce = [s1[0:1] + s2], [i1[0:1] * float(PEER_NKEYS) + i2]
    for a in range(1, SUBLANES):
        keep = bid < (k // (a + 1))
        cs.append(jnp.where(keep, s1[a:a + 1] + s2[0:SUBLANES], -jnp.inf))
        ce.append(i1[a:a + 1] * float(PEER_NKEYS) + i2[0:SUBLANES])
    cs.append(s1[SUBLANES:k] + s2[0:1])
    ce.append(i1[SUBLANES:k] * float(PEER_NKEYS) + i2[0:1])
    return jnp.concatenate(cs, axis=0), jnp.concatenate(ce, axis=0)


def _route_body(x_ref, mod_ref, g_ref, wq_ref, keys_ref, h_ref, idx_ref, gate_ref):
    m = mod_ref[0]
    h2 = _norm_mod(x_ref[...], g_ref[...], m[3:4], m[4:5])
    h_ref[...] = h2
    q = jnp.dot(h2.astype(jnp.bfloat16), wq_ref[...], preferred_element_type=jnp.float32)
    idx_rows, gate_rows = [], []
    for hd in range(PEER_HEADS):
        qh = q[:, hd * PEER_DKEY:(hd + 1) * PEER_DKEY].astype(jnp.bfloat16)
        tops = []
        for s in range(2):
            st = lax.dot_general(keys_ref[hd, s], qh, (((1,), (1,)), ((), ())),
                                 preferred_element_type=jnp.float32)
            tops.append(_top16_rows(st))
        (s1, i1), (s2, i2) = tops
        best_s, best_e = _top16_rows(*_pair_candidates(s1, i1, s2, i2))
        ex = jnp.exp(best_s - best_s[0:1])
        gate_rows.append(ex / jnp.sum(ex, axis=0, keepdims=True))
        idx_rows.append(best_e)
    idx_ref[...] = jnp.concatenate(idx_rows, axis=0).T.astype(jnp.int32)
    gate_ref[...] = jnp.concatenate(gate_rows, axis=0).T


def _route(xs, modl, g2, wq_bf, keys_pad, blocks_per_batch, ctx_blocks):
    t, d = xs.shape
    tb = PEER_ROUTE_BLOCK
    row = lambda i: (i, 0)
    return pl.pallas_call(
        _route_body,
        grid=(t // tb,),
        in_specs=[pl.BlockSpec((tb, d), row),
                  pl.BlockSpec((1, 6, d), lambda i: (_mod_row(i, blocks_per_batch, ctx_blocks), 0, 0)),
                  pl.BlockSpec((1, d), lambda i: (0, 0)),
                  pl.BlockSpec((d, PEER_HEADS * PEER_DKEY), lambda i: (0, 0)),
                  pl.BlockSpec(keys_pad.shape, lambda i: (0, 0, 0, 0))],
        out_specs=[pl.BlockSpec((tb, d), row), pl.BlockSpec((tb, PICKS), row), pl.BlockSpec((tb, PICKS), row)],
        out_shape=[jax.ShapeDtypeStruct((t, d), jnp.float32),
                   jax.ShapeDtypeStruct((t, PICKS), jnp.int32), jax.ShapeDtypeStruct((t, PICKS), jnp.float32)],
        compiler_params=_cparams(("arbitrary",)),
        name="peer_route",
    )(xs, modl, g2, wq_bf, keys_pad)


def _pack_table(tab):
    n, d = tab.shape
    bits = lax.bitcast_convert_type(tab.astype(jnp.bfloat16), jnp.uint16).astype(jnp.uint32)
    word = (bits[:, :d // 2] << 16) | bits[:, d // 2:]
    return lax.bitcast_convert_type(word, jnp.int32).reshape(n, d // 2 // LANES, LANES)


_NT = (((1,), (1,)), ((), ()))
GATHER_PARTS = 4
IDX_OFFSETS = 8


def _hi_lo_rows(first, second, rid):
    f_hi = first.astype(jnp.bfloat16).astype(jnp.float32)
    s_hi = second.astype(jnp.bfloat16).astype(jnp.float32)
    v = jnp.where(rid == 0, f_hi, jnp.where(rid == 1, first - f_hi,
                  jnp.where(rid == 2, s_hi, jnp.where(rid == 3, second - s_hi, 0.0))))
    return v.astype(jnp.bfloat16)


def _gather_part(idx_ref, offs, tab_ref, st_ref, slot, t, part):
    n = PICKS // GATHER_PARTS
    nrow = tab_ref.shape[1]
    k = len(offs)
    for p0 in range(part * n, (part + 1) * n, k):
        sub = idx_ref.at[t, pl.ds(p0, k)]
        for j in range(k):
            st_ref[slot, pl.ds(p0 + j, nrow, stride=SLAB_STRIDE), :] = tab_ref[sub[offs[j]]]


def _slab(st_ref, slot, r):
    return pltpu.bitcast(st_ref[slot, r * SLAB_STRIDE:r * SLAB_STRIDE + PICKS, :], jnp.bfloat16)


def _token_pipeline(n_tok, gather_part, compute_part, finish):
    g = PEER_GROUP

    def step(slot, t, prev_slot, t_prev):
        acc = None
        for part in range(GATHER_PARTS):
            gather_part(slot, t, part)
            if prev_slot is not None:
                acc = compute_part(prev_slot, t_prev, part, acc)
        if prev_slot is not None:
            finish(t_prev, acc)

    step(0, 0, None, None)
    for j in range(1, g):
        step(j, j, j - 1, j - 1)

    def group(gi, carry):
        t0 = gi * g
        step(0, t0, g - 1, t0 - 1)
        for j in range(1, g):
            step(j, t0 + j, j - 1, t0 + j - 1)
        return carry

    lax.fori_loop(1, n_tok // g, group, 0)
    acc = None
    for part in range(GATHER_PARTS):
        acc = compute_part(g - 1, n_tok - 1, part, acc)
    finish(n_tok - 1, acc)


def _expert_act_body(idx_ref, off_ref, h_ref, gate_ref, tab_ref, w_ref, st_ref):
    rid = lax.broadcasted_iota(jnp.int32, (SUBLANES, LANES), 0)
    lane = lax.broadcasted_iota(jnp.int32, (1, 2 * PICKS), 1)
    even = (lane % 2) == 0

    def compute_part(slot, t, r, acc):
        half = h_ref.shape[1] // 2
        hrow = h_ref[pl.ds(t, 1), :]
        xh = jnp.broadcast_to(hrow[:, r * LANES:(r + 1) * LANES], rid.shape)
        xl = jnp.broadcast_to(hrow[:, half + r * LANES:half + (r + 1) * LANES], rid.shape)
        res = lax.dot_general(_hi_lo_rows(xh, xl, rid), _slab(st_ref, slot, r), _NT,
                              preferred_element_type=jnp.float32)
        return res if acc is None else acc + res

    def finish(t, acc):
        v = jnp.where(even, acc[2:3] + acc[3:4], acc[0:1] + acc[1:2])
        act = v + jnp.where(even, pltpu.roll(v, 2 * PICKS - 1, axis=1), pltpu.roll(v, 1, axis=1))
        gl = 0.5 * act * (1.0 + lax.erf(act * (1.0 / math.sqrt(2.0))))
        w_ref[pl.ds(t, 1), :] = gate_ref[pl.ds(t, 1), :] * gl

    offs = [off_ref[j] for j in range(IDX_OFFSETS)]
    _token_pipeline(h_ref.shape[0], partial(_gather_part, idx_ref, offs, tab_ref, st_ref), compute_part, finish)


def _expert_out_body(idx_ref, off_ref, w_ref, tab_ref, o_ref, st_ref):
    rid = lax.broadcasted_iota(jnp.int32, (SUBLANES, 2 * PICKS), 0)
    lane = lax.broadcasted_iota(jnp.int32, (SUBLANES, 2 * PICKS), 1)
    even = (lane % 2) == 0

    def compute_part(slot, t, r, acc):
        wrow = jnp.broadcast_to(w_ref[pl.ds(t, 1), :], rid.shape)
        lhs = _hi_lo_rows(jnp.where(even, 0.0, wrow), jnp.where(even, wrow, 0.0), rid)
        res = jnp.dot(lhs, _slab(st_ref, slot, r), preferred_element_type=jnp.float32)
        first, second = acc if acc is not None else ([], [])
        return first + [res[0:1] + res[1:2]], second + [res[2:3] + res[3:4]]

    def finish(t, acc):
        o_ref[pl.ds(t, 1), :] = jnp.concatenate(acc[0] + acc[1], axis=1)

    offs = [off_ref[j] for j in range(IDX_OFFSETS)]
    _token_pipeline(o_ref.shape[0], partial(_gather_part, idx_ref, offs, tab_ref, st_ref), compute_part, finish)


def _expert_specs(t):
    nt = PEER_TOK_BLOCK
    nrow = D_MODEL // 2 // LANES
    assert nrow == GATHER_PARTS and nt % PEER_GROUP == 0 and nt // PEER_GROUP >= 2
    smem = pl.BlockSpec((nt, PICKS), lambda i: (i, 0), memory_space=pltpu.SMEM)
    vrow = pl.BlockSpec((nt, 2 * PICKS), lambda i: (i, 0))
    full = pl.BlockSpec((nt, D_MODEL), lambda i: (i, 0))
    table = pl.BlockSpec(memory_space=pltpu.VMEM)
    stage = pltpu.VMEM((PEER_GROUP, nrow * SLAB_STRIDE, LANES), jnp.int32)
    assert PICKS // GATHER_PARTS % IDX_OFFSETS == 0
    offs = pl.BlockSpec(memory_space=pltpu.SMEM)
    return nt, smem, offs, vrow, full, table, stage


def _expert_act(idx, h2, gate2, tab_u):
    t = idx.shape[0]
    nt, smem, offs, vrow, full, table, stage = _expert_specs(t)
    return pl.pallas_call(
        _expert_act_body,
        grid=(t // nt,),
        in_specs=[smem, offs, full, vrow, table],
        out_specs=vrow,
        out_shape=jax.ShapeDtypeStruct((t, 2 * PICKS), jnp.float32),
        scratch_shapes=[stage],
        compiler_params=_cparams(("arbitrary",)),
        name="peer_act",
    )(idx, jnp.arange(IDX_OFFSETS, dtype=jnp.int32), h2, gate2, tab_u)


def _expert_out(idx, w2, tab_v):
    t = idx.shape[0]
    nt, smem, offs, vrow, full, table, stage = _expert_specs(t)
    return pl.pallas_call(
        _expert_out_body,
        grid=(t // nt,),
        in_specs=[smem, offs, vrow, table],
        out_specs=full,
        out_shape=jax.ShapeDtypeStruct((t, D_MODEL), jnp.float32),
        scratch_shapes=[stage],
        compiler_params=_cparams(("arbitrary",)),
        name="peer_out",
    )(idx, jnp.arange(IDX_OFFSETS, dtype=jnp.int32), w2, tab_v)


def _residual_body(x_ref, p_ref, mod_ref, o_ref):
    o_ref[...] = x_ref[...] + mod_ref[0][5:6] * p_ref[...]


def _residual(xs, po, modl, blocks_per_batch, ctx_blocks):
    t, d = xs.shape
    tb = TOK_BLOCK
    row = lambda i: (i, 0)
    return pl.pallas_call(
        _residual_body,
        grid=(t // tb,),
        in_specs=[pl.BlockSpec((tb, d), row), pl.BlockSpec((tb, d), row),
                  pl.BlockSpec((1, 6, d), lambda i: (_mod_row(i, blocks_per_batch, ctx_blocks), 0, 0))],
        out_specs=pl.BlockSpec((tb, d), row),
        out_shape=jax.ShapeDtypeStruct((t, d), jnp.float32),
        compiler_params=_cparams(("arbitrary",)),
        name="peer_residual",
    )(xs, po, modl)


def _final_norm_body(x_ref, g_ref, o_ref):
    xf = x_ref[...]
    o_ref[...] = xf * lax.rsqrt(jnp.mean(xf * xf, axis=-1, keepdims=True) + EPS) * g_ref[...]


def _final_norm(x2, g):
    rows, d = x2.shape
    tm = TOK_BLOCK
    return pl.pallas_call(
        _final_norm_body,
        grid=(rows // tm,),
        in_specs=[pl.BlockSpec((tm, d), lambda i: (i, 0)), pl.BlockSpec((1, d), lambda i: (0, 0))],
        out_specs=pl.BlockSpec((tm, d), lambda i: (i, 0)),
        out_shape=jax.ShapeDtypeStruct((rows, d), x2.dtype),
        compiler_params=_cparams(("arbitrary",)),
        name="final_norm",
    )(x2, g.reshape(1, d))


def _rope_tables(seq, lc, nb):
    rows = seq // GRID_W
    row_ids = jnp.repeat(jnp.arange(rows, dtype=jnp.float32), GRID_W)
    col_ids = jnp.tile(jnp.arange(GRID_W, dtype=jnp.float32), rows)
    n_freq = RET_HEAD_DIM // 4
    freqs = ROPE_BASE ** (-jnp.arange(n_freq, dtype=jnp.float32) / n_freq)
    ang_r, ang_c = row_ids[:, None] * freqs[None], col_ids[:, None] * freqs[None]
    cos = jnp.concatenate([jnp.cos(ang_r), jnp.cos(ang_r), jnp.cos(ang_c), jnp.cos(ang_c)], axis=-1)
    sin = jnp.concatenate([-jnp.sin(ang_r), jnp.sin(ang_r), -jnp.sin(ang_c), jnp.sin(ang_c)], axis=-1)
    cos = jnp.concatenate([jnp.ones((lc, RET_HEAD_DIM), jnp.float32), cos], axis=0)
    sin = jnp.concatenate([jnp.zeros((lc, RET_HEAD_DIM), jnp.float32), sin], axis=0)
    return jnp.tile(cos, (nb, 1)), jnp.tile(sin, (nb, 1))


def kernel(x, c, ctx, c_ctx, w_mod, b_mod, norm1_g, norm2_g, w_in, ssm_B_re, ssm_B_im,
           ssm_C_re, ssm_C_im, ssm_D, ssm_lam_re_f, ssm_lam_im_f, ssm_log_dt_f,
           ssm_lam_re_b, ssm_lam_im_b, ssm_log_dt_b, w_ssm_glu, ret_decay_f, ret_decay_b,
           w_ret_up, w_out, peer_w_q, peer_sub_keys, peer_u, peer_v, final_norm_g):
    nb, seq, d = x.shape
    lc = ctx.shape[1]
    depth = w_mod.shape[0]
    lb = lc + seq
    bf = jnp.bfloat16
    assert nb == 2 and d == D_MODEL and lc % TOK_BLOCK == 0 and seq % TOK_BLOCK == 0
    bpb, cbl = lb // TOK_BLOCK, lc // TOK_BLOCK
    bpb_r, cbl_r = lb // PEER_ROUTE_BLOCK, lc // PEER_ROUTE_BLOCK

    xs = jnp.concatenate([ctx, x], axis=1).reshape(nb * lb, d)
    cvec = jnp.zeros((SUBLANES, d), jnp.float32).at[:nb].set(c).at[nb].set(c_ctx)
    mod_all = _modulation(cvec, w_mod, b_mod)[:, :nb + 1].reshape(depth, nb + 1, 6, d)
    cos_t, sin_t = _rope_tables(seq, lc, nb)
    half = PEER_DKEY // 2

    for i in range(depth):
        modl = mod_all[i]
        s5_ops = _s5_operators(ssm_B_re[i], ssm_B_im[i], ssm_C_re[i], ssm_C_im[i], ssm_D[i],
                               ssm_lam_re_f[i], ssm_lam_im_f[i], ssm_log_dt_f[i],
                               ssm_lam_re_b[i], ssm_lam_im_b[i], ssm_log_dt_b[i])
        ret_c = _ret_consts(ret_decay_f[i], ret_decay_b[i])
        u, q, k, v, g, gs, gr = _in_proj(xs, modl, norm1_g[i].reshape(1, d), w_in[i].astype(bf),
                                         cos_t, sin_t, bpb, cbl)
        ys = _s5_mix(u, s5_ops, nb, lb, lc)
        yr = _retention(q, k, v, g, ret_c, nb, lb, lc)
        xs = _merge(xs, ys, yr, gs, gr, modl, w_ssm_glu[i].astype(bf), w_ret_up[i].astype(bf),
                    w_out[i].astype(bf), bpb, cbl)
        sk = peer_sub_keys[i].astype(bf)
        zeros = jnp.zeros_like(sk)
        keys_pad = jnp.stack([jnp.concatenate([sk[:, 0], zeros[:, 0]], axis=-1),
                              jnp.concatenate([zeros[:, 1], sk[:, 1]], axis=-1)], axis=1)
        h2, idx, gate = _route(xs, modl, norm2_g[i].reshape(1, d), peer_w_q[i].astype(bf),
                                   keys_pad, bpb_r, cbl_r)
        wts = _expert_act(idx, h2, jnp.repeat(gate, 2, axis=1), _pack_table(peer_u[i]))
        xs = _residual(xs, _expert_out(idx, wts, _pack_table(peer_v[i])), modl, bpb, cbl)

    lat = xs.reshape(nb, lb, d)[:, lc:].reshape(nb * seq, d)
    return _final_norm(lat, final_norm_g).reshape(nb, seq, d)
```

```python
import math
from functools import partial

import jax
import jax.numpy as jnp
from jax import lax
from jax.experimental import pallas as pl
from jax.experimental.pallas import tpu as pltpu

D_MODEL = 1024
GRID_W = 64
EPS = 1e-6
SSM_WIDTH = 512
SSM_GROUP = 16
SSM_GROUPS = SSM_WIDTH // SSM_GROUP
SSM_STATE = 64
RET_WIDTH = 512
RET_HEADS = 4
RET_HEAD_DIM = RET_WIDTH // RET_HEADS
ROPE_BASE = 10000.0
PEER_HEADS = 8
PEER_NKEYS = 128
PEER_DKEY = 128
PEER_TOPK = 16
IN_COLS = SSM_WIDTH + 4 * RET_WIDTH + 2 * D_MODEL

LANES = 128
SUBLANES = 8
VMEM_LIMIT = 56 * 1024 * 1024
TOK_BLOCK = 256
RET_CHUNK = 128
S5_CHUNK = 64
PEER_ROUTE_BLOCK = 128
PEER_TOK_BLOCK = 256
PEER_GROUP = 16
PICKS = PEER_HEADS * PEER_TOPK
SLAB_STRIDE = LANES + SUBLANES


def _cparams(sem=None):
    return pltpu.CompilerParams(dimension_semantics=sem, vmem_limit_bytes=VMEM_LIMIT)


def _mod_body(c_ref, w_ref, b_ref, o_ref):
    cv = c_ref[...]
    s = cv * jax.nn.sigmoid(cv)
    o_ref[0] = jnp.dot(s, w_ref[0], preferred_element_type=jnp.float32,
                       precision=lax.Precision.HIGHEST) + b_ref[0]


def _modulation(cvec, w_mod, b_mod):
    depth, d, d6 = w_mod.shape
    nt = d6 // d
    return pl.pallas_call(
        _mod_body,
        grid=(depth, nt),
        in_specs=[pl.BlockSpec((SUBLANES, d), lambda l, j: (0, 0)),
                  pl.BlockSpec((1, d, d), lambda l, j: (l, 0, j)),
                  pl.BlockSpec((1, 1, d), lambda l, j: (l, 0, j))],
        out_specs=pl.BlockSpec((1, SUBLANES, d), lambda l, j: (l, 0, j)),
        out_shape=jax.ShapeDtypeStruct((depth, SUBLANES, d6), jnp.float32),
        compiler_params=_cparams(("arbitrary", "arbitrary")),
        name="adaln_mod",
    )(cvec, w_mod, b_mod.reshape(depth, 1, d6))


def _mod_row(i, blocks_per_batch, ctx_blocks):
    b = i // blocks_per_batch
    j = i - b * blocks_per_batch
    return jnp.where(j < ctx_blocks, 2, b)


def _norm_mod(x, g, shift, scale):
    y = x * lax.rsqrt(jnp.mean(x * x, axis=-1, keepdims=True) + EPS)
    return (y * g) * (1.0 + scale) + shift


def _swap_halves(t):
    lane = lax.broadcasted_iota(jnp.int32, t.shape, 1)
    first = (lane % 64) < 32
    return jnp.where(first, pltpu.roll(t, 96, axis=1), pltpu.roll(t, 32, axis=1))


def _in_body(x_ref, mod_ref, g_ref, w_ref, cos_ref, sin_ref,
             u_ref, q_ref, k_ref, v_ref, gg_ref, gs_ref, gr_ref):
    m = mod_ref[0]
    h = _norm_mod(x_ref[...], g_ref[...], m[0:1], m[1:2]).astype(jnp.bfloat16)

    def proj(lo, hi):
        return jnp.dot(h, w_ref[:, lo:hi], preferred_element_type=jnp.float32)

    o = 0
    u_ref[...] = proj(o, o + SSM_WIDTH).astype(u_ref.dtype)
    o += SSM_WIDTH
    cos = cos_ref[...]
    sin = sin_ref[...]
    k_scale = RET_HEAD_DIM ** -0.5
    for dst, scl in ((q_ref, 1.0), (k_ref, k_scale)):
        t = proj(o, o + RET_WIDTH)
        for hd in range(RET_HEADS):
            th = t[:, hd * RET_HEAD_DIM:(hd + 1) * RET_HEAD_DIM]
            r = th * cos + _swap_halves(th) * sin
            if scl != 1.0:
                r = r * scl
            dst[:, hd * RET_HEAD_DIM:(hd + 1) * RET_HEAD_DIM] = r.astype(dst.dtype)
        o += RET_WIDTH
    v_ref[...] = proj(o, o + RET_WIDTH).astype(v_ref.dtype)
    o += RET_WIDTH
    gg_ref[...] = proj(o, o + RET_WIDTH).astype(gg_ref.dtype)
    o += RET_WIDTH
    gs_ref[...] = proj(o, o + D_MODEL).astype(gs_ref.dtype)
    o += D_MODEL
    gr_ref[...] = proj(o, o + D_MODEL).astype(gr_ref.dtype)


def _in_proj(xs, modl, g1, w_in_bf, cos_t, sin_t, blocks_per_batch, ctx_blocks):
    t, d = xs.shape
    tb = TOK_BLOCK
    row = lambda i: (i, 0)
    bf = jnp.bfloat16
    outs = [jax.ShapeDtypeStruct((t, SSM_WIDTH), bf)] + [jax.ShapeDtypeStruct((t, RET_WIDTH), bf)] * 4 \
        + [jax.ShapeDtypeStruct((t, D_MODEL), bf)] * 2
    return pl.pallas_call(
        _in_body,
        grid=(t // tb,),
        in_specs=[pl.BlockSpec((tb, d), row),
                  pl.BlockSpec((1, 6, d), lambda i: (_mod_row(i, blocks_per_batch, ctx_blocks), 0, 0)),
                  pl.BlockSpec((1, d), lambda i: (0, 0)),
                  pl.BlockSpec((d, IN_COLS), lambda i: (0, 0)),
                  pl.BlockSpec((tb, RET_HEAD_DIM), row),
                  pl.BlockSpec((tb, RET_HEAD_DIM), row)],
        out_specs=[pl.BlockSpec((tb, SSM_WIDTH), row)] + [pl.BlockSpec((tb, RET_WIDTH), row)] * 4
        + [pl.BlockSpec((tb, D_MODEL), row)] * 2,
        out_shape=outs,
        compiler_params=_cparams(("arbitrary",)),
        name="in_proj",
    )(xs, modl, g1, w_in_bf, cos_t, sin_t)


def _s5_operators(B_re, B_im, C_re, C_im, d_skip, lam_re_f, lam_im_f, log_dt_f,
                  lam_re_b, lam_im_b, log_dt_b):
    f32 = jnp.float32
    hp = lax.Precision.HIGHEST
    T = S5_CHUNK
    ks = jnp.arange(T + 1, dtype=f32)[:, None, None]

    def direction(lam_re, lam_im, log_dt):
        dt = jnp.exp(log_dt.astype(f32))[:, None]
        ar, ai = lam_re.astype(f32) * dt, lam_im.astype(f32) * dt
        mag = jnp.exp(ks * ar)
        pw_re, pw_im = mag * jnp.cos(ks * ai), mag * jnp.sin(ks * ai)
        x, y = pw_re[1] - 1.0, pw_im[1]
        den = lam_re * lam_re + lam_im * lam_im
        bf_re, bf_im = (x * lam_re + y * lam_im) / den, (y * lam_re - x * lam_im) / den
        bt_re = bf_re[..., None] * B_re - bf_im[..., None] * B_im
        bt_im = bf_re[..., None] * B_im + bf_im[..., None] * B_re
        cp_re = C_re[None] * pw_re[:, :, None, :] - C_im[None] * pw_im[:, :, None, :]
        cp_im = C_re[None] * pw_im[:, :, None, :] + C_im[None] * pw_re[:, :, None, :]
        taps = (jnp.einsum('kgjp,gpi->kgij', cp_re[:T], bt_re, precision=hp)
                - jnp.einsum('kgjp,gpi->kgij', cp_im[:T], bt_im, precision=hp))
        zw_re = pw_re[:T, :, :, None] * bt_re[None] - pw_im[:T, :, :, None] * bt_im[None]
        zw_im = pw_re[:T, :, :, None] * bt_im[None] + pw_im[:T, :, :, None] * bt_re[None]
        return taps, (zw_re, zw_im), (cp_re, cp_im), (pw_re[T], pw_im[T])

    taps_f, zw_f, cp_f, a_f = direction(lam_re_f, lam_im_f, log_dt_f)
    taps_b, zw_b, cp_b, a_b = direction(lam_re_b, lam_im_b, log_dt_b)
    G, I = SSM_GROUPS, SSM_GROUP
    dd = d_skip.astype(f32).reshape(G, I)
    center = taps_f[0] + taps_b[0] + dd[:, :, None] * jnp.eye(I, dtype=f32)[None]
    full = jnp.concatenate([taps_b[:0:-1], center[None], taps_f[1:]], axis=0)
    full = full.astype(jnp.bfloat16)
    s_i = jnp.arange(T)
    toep = full[(s_i[None, :] - s_i[:, None]) + T - 1]
    m_op = toep.transpose(2, 0, 3, 1, 4).reshape(G, T * I, T * I)

    def zcols(zw, flip):
        re, im = zw
        if flip:
            re, im = re[::-1], im[::-1]
        f = lambda a: a.transpose(1, 0, 3, 2).reshape(G, T * I, SSM_STATE)
        return [f(re), f(im)]

    v_op = jnp.concatenate(zcols(zw_f, True) + zcols(zw_b, False), axis=-1)

    def wrows(cp, idx):
        re, im = cp
        f = lambda a: a[idx].transpose(1, 3, 0, 2).reshape(G, SSM_STATE, T * I)
        return [f(re), -f(im)]

    w_op = jnp.concatenate(wrows(cp_f, jnp.arange(1, T + 1)) + wrows(cp_b, T - jnp.arange(T)), axis=1)
    a1 = jnp.stack([a_f[0], a_f[0], a_b[0], a_b[0]]).reshape(4, G * SSM_STATE)
    a2 = jnp.stack([-a_f[1], a_f[1], -a_b[1], a_b[1]]).reshape(4, G * SSM_STATE)
    return (m_op, v_op.astype(jnp.bfloat16)), w_op.astype(jnp.bfloat16), a1, a2


def _s5_intra_body(u_ref, m_ref, v_ref, y_ref, z_ref):
    u = u_ref[0]
    y_ref[0] = jnp.dot(u, m_ref[0], preferred_element_type=jnp.float32)
    z_ref[0] = jnp.dot(u, v_ref[0], preferred_element_type=jnp.float32)


def _s5_intra(ug, mv):
    m_op, v_op = mv
    g, m, kdim = ug.shape
    n, nz = m_op.shape[-1], v_op.shape[-1]
    return pl.pallas_call(
        _s5_intra_body,
        grid=(g,),
        in_specs=[pl.BlockSpec((1, m, kdim), lambda i: (i, 0, 0)),
                  pl.BlockSpec((1, kdim, n), lambda i: (i, 0, 0)),
                  pl.BlockSpec((1, kdim, nz), lambda i: (i, 0, 0))],
        out_specs=[pl.BlockSpec((1, m, n), lambda i: (i, 0, 0)),
                   pl.BlockSpec((1, m, nz), lambda i: (i, 0, 0))],
        out_shape=[jax.ShapeDtypeStruct((g, m, n), jnp.float32),
                   jax.ShapeDtypeStruct((g, m, nz), jnp.float32)],
        compiler_params=_cparams(("arbitrary",)),
        name="s5_intra",
    )(ug, m_op, v_op)


def _s5_scan_body(z_ref, a1_ref, a2_ref, s_ref):
    steps = z_ref.shape[0]
    a1 = a1_ref[...]
    a2 = a2_ref[...]
    row = lax.broadcasted_iota(jnp.int32, a1.shape, 0)
    even = (row % 2) == 0

    def step(k, s):
        s_ref[k] = s
        partner = jnp.where(even, pltpu.roll(s, SUBLANES - 1, axis=0), pltpu.roll(s, 1, axis=0))
        return a1 * s + a2 * partner + z_ref[k]

    lax.fori_loop(0, steps, step, jnp.zeros(a1.shape, jnp.float32))


def _s5_scan(z8, a1, a2):
    steps, r, n = z8.shape
    cb = 512
    return pl.pallas_call(
        _s5_scan_body,
        grid=(n // cb,),
        in_specs=[pl.BlockSpec((steps, r, cb), lambda i: (0, 0, i)),
                  pl.BlockSpec((r, cb), lambda i: (0, i)),
                  pl.BlockSpec((r, cb), lambda i: (0, i))],
        out_specs=pl.BlockSpec((steps, r, cb), lambda i: (0, 0, i)),
        out_shape=jax.ShapeDtypeStruct((steps, r, n), jnp.float32),
        compiler_params=_cparams(("arbitrary",)),
        name="s5_scan",
    )(z8, a1, a2)


def _s5_out_body(y_ref, s_ref, w_ref, o_ref):
    o_ref[0] = (y_ref[0] + jnp.dot(s_ref[0], w_ref[0], preferred_element_type=jnp.float32)).astype(o_ref.dtype)


def _s5_out(yi, sg, w_op):
    g, m, n = yi.shape
    ks = sg.shape[-1]
    return pl.pallas_call(
        _s5_out_body,
        grid=(g,),
        in_specs=[pl.BlockSpec((1, m, n), lambda i: (i, 0, 0)),
                  pl.BlockSpec((1, m, ks), lambda i: (i, 0, 0)),
                  pl.BlockSpec((1, ks, n), lambda i: (i, 0, 0))],
        out_specs=pl.BlockSpec((1, m, n), lambda i: (i, 0, 0)),
        out_shape=jax.ShapeDtypeStruct((g, m, n), jnp.bfloat16),
        compiler_params=_cparams(("arbitrary",)),
        name="s5_out",
    )(yi, sg, w_op)


def _s5_mix(u, ops, nb, lb, lc):
    mv, w_op, a1, a2 = ops
    G, I, P, T = SSM_GROUPS, SSM_GROUP, SSM_STATE, S5_CHUNK
    nch = lb // T
    cch = lc // T
    ug = u.reshape(nb * nch, T, G, I).transpose(2, 0, 1, 3).reshape(G, nb * nch, T * I)
    yi, z = _s5_intra(ug, mv)
    order_b = jnp.concatenate([jnp.arange(cch - 1, -1, -1), jnp.arange(nch - 1, cch - 1, -1)])
    z6 = z.reshape(G, nb, nch, 2, 2, P)
    zf = z6[:, :, :, 0]
    zb = z6[:, :, order_b, 1]
    z8 = jnp.stack([zf, zb], axis=3)
    z8 = z8.transpose(2, 1, 3, 4, 0, 5).reshape(nch, nb * 4, G * P)
    rows = nb * 4
    pad = (-rows) % SUBLANES
    reps = (rows + pad) // 4
    if pad:
        z8 = jnp.pad(z8, ((0, 0), (0, pad), (0, 0)))
    s8 = _s5_scan(z8, jnp.tile(a1, (reps, 1)), jnp.tile(a2, (reps, 1)))[:, :rows]
    s6 = s8.reshape(nch, nb, 2, 2, G, P).transpose(4, 1, 0, 2, 3, 5)
    inv_b = jnp.argsort(order_b)
    sf = s6[:, :, :, 0]
    sb = s6[:, :, inv_b, 1]
    sg = jnp.stack([sf, sb], axis=3).reshape(G, nb * nch, 4 * P).astype(jnp.bfloat16)
    y = _s5_out(yi, sg, w_op)
    return y.reshape(G, nb * nch, T, I).transpose(1, 2, 0, 3).reshape(nb * lb, G * I)


def _ret_consts(ret_decay_f, ret_decay_b):
    f32 = jnp.float32
    C = RET_CHUNK
    lg_f = -jnp.exp(ret_decay_f.astype(f32))[:, None, None]
    lg_b = -jnp.exp(ret_decay_b.astype(f32))[:, None, None]
    pos = jnp.arange(C, dtype=f32)
    diff = pos[:, None] - pos[None, :]
    dmat = jnp.where(diff >= 0, jnp.exp(jnp.where(diff >= 0, diff, 0.0)[None] * lg_f),
                     jnp.exp(jnp.where(diff < 0, -diff, 0.0)[None] * lg_b))
    col = lambda e: jnp.broadcast_to(jnp.exp(e), (RET_HEADS, C, RET_HEAD_DIM))
    p1 = pos[None, :, None]
    qdec_f = col((p1 + 1.0) * lg_f)
    kdec_f = col((C - 1.0 - p1) * lg_f)
    qdec_b = col((C - p1) * lg_b)
    kdec_b = col(p1 * lg_b)
    cd_f = col(jnp.full_like(p1, C) * lg_f)
    cd_b = col(jnp.full_like(p1, C) * lg_b)
    return dmat, qdec_f, kdec_f, cd_f, qdec_b, kdec_b, cd_b


def _ret_state_update(s_ref, b, hd, kh, vh, kdec, cd):
    kd = (kh.astype(jnp.float32) * kdec).astype(jnp.bfloat16)
    inc = lax.dot_general(kd, vh, (((0,), (0,)), ((), ())), preferred_element_type=jnp.float32)
    s_ref[b, hd] = cd * s_ref[b, hd] + inc


def _ret_fwd_body(q_ref, k_ref, v_ref, dm_ref, qd_ref, kd_ref, cd_ref, o_ref, s_ref):
    @pl.when(pl.program_id(0) == 0)
    def _():
        s_ref[...] = jnp.zeros(s_ref.shape, s_ref.dtype)

    for b in range(q_ref.shape[0]):
        for hd in range(RET_HEADS):
            sl = slice(hd * RET_HEAD_DIM, (hd + 1) * RET_HEAD_DIM)
            qh, kh, vh = q_ref[b, :, sl], k_ref[b, :, sl], v_ref[b, :, sl]
            sc = lax.dot_general(qh, kh, (((1,), (1,)), ((), ())), preferred_element_type=jnp.float32)
            p = (sc * dm_ref[hd]).astype(jnp.bfloat16)
            o = jnp.dot(p, vh, preferred_element_type=jnp.float32)
            cross = jnp.dot(qh, s_ref[b, hd].astype(jnp.bfloat16), preferred_element_type=jnp.float32)
            o_ref[b, :, sl] = o + cross * qd_ref[hd]
            _ret_state_update(s_ref, b, hd, kh, vh, kd_ref[hd], cd_ref[hd])


def _ret_bwd_body(q_ref, k_ref, v_ref, of_ref, g_ref, qd_ref, kd_ref, cd_ref, y_ref, s_ref):
    @pl.when(pl.program_id(0) == 0)
    def _():
        s_ref[...] = jnp.zeros(s_ref.shape, s_ref.dtype)

    for b in range(q_ref.shape[0]):
        for hd in range(RET_HEADS):
            sl = slice(hd * RET_HEAD_DIM, (hd + 1) * RET_HEAD_DIM)
            qh, kh, vh = q_ref[b, :, sl], k_ref[b, :, sl], v_ref[b, :, sl]
            cross = jnp.dot(qh, s_ref[b, hd].astype(jnp.bfloat16), preferred_element_type=jnp.float32)
            o = of_ref[b, :, sl] + cross * qd_ref[hd]
            o = o * lax.rsqrt(jnp.mean(o * o, axis=-1, keepdims=True) + EPS)
            gv = g_ref[b, :, sl].astype(jnp.float32)
            y_ref[b, :, sl] = (o * (gv * jax.nn.sigmoid(gv))).astype(y_ref.dtype)
            _ret_state_update(s_ref, b, hd, kh, vh, kd_ref[hd], cd_ref[hd])


def _retention(q, k, v, g, consts, nb, lb, lc):
    dmat, qdec_f, kdec_f, cd_f, qdec_b, kdec_b, cd_b = consts
    C, W = RET_CHUNK, RET_WIDTH
    nblk, cblk = lb // C, lc // C
    r3 = lambda a: a.reshape(nb, lb, W)
    fwd_idx = lambda s: (0, s, 0)
    bwd_idx = lambda s: (0, jnp.where(s < cblk, cblk - 1 - s, nblk - 1 + cblk - s), 0)
    cst = pl.BlockSpec((RET_HEADS, C, RET_HEAD_DIM), lambda s: (0, 0, 0))
    state = pltpu.VMEM((nb, RET_HEADS, RET_HEAD_DIM, RET_HEAD_DIM), jnp.float32)
    o_f = pl.pallas_call(
        _ret_fwd_body,
        grid=(nblk,),
        in_specs=[pl.BlockSpec((nb, C, W), fwd_idx)] * 3 + [cst] * 4,
        out_specs=pl.BlockSpec((nb, C, W), fwd_idx),
        out_shape=jax.ShapeDtypeStruct((nb, lb, W), jnp.float32),
        scratch_shapes=[state],
        compiler_params=_cparams(("arbitrary",)),
        name="ret_fwd",
    )(r3(q), r3(k), r3(v), dmat, qdec_f, kdec_f, cd_f)
    y = pl.pallas_call(
        _ret_bwd_body,
        grid=(nblk,),
        in_specs=[pl.BlockSpec((nb, C, W), bwd_idx)] * 5 + [cst] * 3,
        out_specs=pl.BlockSpec((nb, C, W), bwd_idx),
        out_shape=jax.ShapeDtypeStruct((nb, lb, W), jnp.bfloat16),
        scratch_shapes=[state],
        compiler_params=_cparams(("arbitrary",)),
        name="ret_bwd",
    )(r3(q), r3(k), r3(v), o_f, r3(g), qdec_b, kdec_b, cd_b)
    return y.reshape(nb * lb, W)


def _gelu_tanh(x):
    return 0.5 * x * (1.0 + jnp.tanh(math.sqrt(2.0 / math.pi) * (x + 0.044715 * x * x * x)))


def _merge_body(x_ref, ys_ref, yr_ref, gs_ref, gr_ref, mod_ref, wg_ref, wr_ref, wo_ref, o_ref):
    f32 = jnp.float32
    a_in = _gelu_tanh(ys_ref[...].astype(f32)).astype(jnp.bfloat16)
    ab = jnp.dot(a_in, wg_ref[...], preferred_element_type=f32)
    ys = ab[:, :D_MODEL] * jax.nn.sigmoid(ab[:, D_MODEL:])
    yr = jnp.dot(yr_ref[...], wr_ref[...], preferred_element_type=f32)
    m = jax.nn.sigmoid(gs_ref[...].astype(f32)) * ys + jax.nn.sigmoid(gr_ref[...].astype(f32)) * yr
    y = jnp.dot(m.astype(jnp.bfloat16), wo_ref[...], preferred_element_type=f32)
    o_ref[...] = x_ref[...] + mod_ref[0][2:3] * y


def _merge(xs, ys, yr, gs, gr, modl, wg, wr, wo, blocks_per_batch, ctx_blocks):
    t, d = xs.shape
    tb = TOK_BLOCK
    row = lambda i: (i, 0)
    full = lambda a: pl.BlockSpec(a.shape, lambda i: (0, 0))
    return pl.pallas_call(
        _merge_body,
        grid=(t // tb,),
        in_specs=[pl.BlockSpec((tb, d), row), pl.BlockSpec((tb, SSM_WIDTH), row),
                  pl.BlockSpec((tb, RET_WIDTH), row), pl.BlockSpec((tb, d), row), pl.BlockSpec((tb, d), row),
                  pl.BlockSpec((1, 6, d), lambda i: (_mod_row(i, blocks_per_batch, ctx_blocks), 0, 0)),
                  full(wg), full(wr), full(wo)],
        out_specs=pl.BlockSpec((tb, d), row),
        out_shape=jax.ShapeDtypeStruct((t, d), jnp.float32),
        compiler_params=_cparams(("arbitrary",)),
        name="merge_out",
    )(xs, ys, yr, gs, gr, modl, wg, wr, wo)


def _top16_rows(vals, payload=None):
    n = vals.shape[0]
    rid = lax.broadcasted_iota(jnp.int32, vals.shape, 0).astype(jnp.float32)
    top_v, top_p = [], []
    for _ in range(PEER_TOPK):
        m = jnp.max(vals, axis=0, keepdims=True)
        first = jnp.min(jnp.where(vals == m, rid, float(n)), axis=0, keepdims=True)
        hit = rid == first
        top_v.append(m)
        top_p.append(first if payload is None else
                     jnp.max(jnp.where(hit, payload, -1.0), axis=0, keepdims=True))
        vals = jnp.where(hit, -jnp.inf, vals)
    return jnp.concatenate(top_v, axis=0), jnp.concatenate(top_p, axis=0)


def _pair_candidates(s1, i1, s2, i2):
    k = PEER_TOPK
    bid = lax.broadcasted_iota(jnp.int32, (SUBLANES, s1.shape[1]), 0)
    cs, ce = [s1[0:1] + s2], [i1[0:1] * float(PEER_NKEYS) + i2]
    for a in range(1, SUBLANES):
        keep = bid < (k // (a + 1))
        cs.append(jnp.where(keep, s1[a:a + 1] + s2[0:SUBLANES], -jnp.inf))
        ce.append(i1[a:a + 1] * float(PEER_NKEYS) + i2[0:SUBLANES])
    cs.append(s1[SUBLANES:k] + s2[0:1])
    ce.append(i1[SUBLANES:k] * float(PEER_NKEYS) + i2[0:1])
    return jnp.concatenate(cs, axis=0), jnp.concatenate(ce, axis=0)


def _route_body(x_ref, mod_ref, g_ref, wq_ref, keys_ref, h_ref, idx_ref, gate_ref):
    m = mod_ref[0]
    h2 = _norm_mod(x_ref[...], g_ref[...], m[3:4], m[4:5])
    h_ref[...] = h2
    q = jnp.dot(h2.astype(jnp.bfloat16), wq_ref[...], preferred_element_type=jnp.float32)
    idx_rows, gate_rows = [], []
    for hd in range(PEER_HEADS):
        qh = q[:, hd * PEER_DKEY:(hd + 1) * PEER_DKEY].astype(jnp.bfloat16)
        tops = []
        for s in range(2):
            st = lax.dot_general(keys_ref[hd, s], qh, (((1,), (1,)), ((), ())),
                                 preferred_element_type=jnp.float32)
            tops.append(_top16_rows(st))
        (s1, i1), (s2, i2) = tops
        best_s, best_e = _top16_rows(*_pair_candidates(s1, i1, s2, i2))
        ex = jnp.exp(best_s - best_s[0:1])
        gate_rows.append(ex / jnp.sum(ex, axis=0, keepdims=True))
        idx_rows.append(best_e)
    idx_ref[...] = jnp.concatenate(idx_rows, axis=0).T.astype(jnp.int32)
    gate_ref[...] = jnp.concatenate(gate_rows, axis=0).T


def _route(xs, modl, g2, wq_bf, keys_pad, blocks_per_batch, ctx_blocks):
    t, d = xs.shape
    tb = PEER_ROUTE_BLOCK
    row = lambda i: (i, 0)
    return pl.pallas_call(
        _route_body,
        grid=(t // tb,),
        in_specs=[pl.BlockSpec((tb, d), row),
                  pl.BlockSpec((1, 6, d), lambda i: (_mod_row(i, blocks_per_batch, ctx_blocks), 0, 0)),
                  pl.BlockSpec((1, d), lambda i: (0, 0)),
                  pl.BlockSpec((d, PEER_HEADS * PEER_DKEY), lambda i: (0, 0)),
                  pl.BlockSpec(keys_pad.shape, lambda i: (0, 0, 0, 0))],
        out_specs=[pl.BlockSpec((tb, d), row), pl.BlockSpec((tb, PICKS), row), pl.BlockSpec((tb, PICKS), row)],
        out_shape=[jax.ShapeDtypeStruct((t, d), jnp.float32),
                   jax.ShapeDtypeStruct((t, PICKS), jnp.int32), jax.ShapeDtypeStruct((t, PICKS), jnp.float32)],
        compiler_params=_cparams(("arbitrary",)),
        name="peer_route",
    )(xs, modl, g2, wq_bf, keys_pad)


def _pack_table(tab):
    n, d = tab.shape
    bits = lax.bitcast_convert_type(tab.astype(jnp.bfloat16), jnp.uint16).astype(jnp.uint32)
    word = (bits[:, :d // 2] << 16) | bits[:, d // 2:]
    return lax.bitcast_convert_type(word, jnp.int32).reshape(n, d // 2 // LANES, LANES)


_NT = (((1,), (1,)), ((), ()))
GATHER_PARTS = 4
IDX_OFFSETS = 8


def _hi_lo_rows(first, second, rid):
    f_hi = first.astype(jnp.bfloat16).astype(jnp.float32)
    s_hi = second.astype(jnp.bfloat16).astype(jnp.float32)
    v = jnp.where(rid == 0, f_hi, jnp.where(rid == 1, first - f_hi,
                  jnp.where(rid == 2, s_hi, jnp.where(rid == 3, second - s_hi, 0.0))))
    return v.astype(jnp.bfloat16)


def _gather_part(idx_ref, offs, tab_ref, st_ref, slot, t, part):
    n = PICKS // GATHER_PARTS
    nrow = tab_ref.shape[1]
    k = len(offs)
    for p0 in range(part * n, (part + 1) * n, k):
        sub = idx_ref.at[t, pl.ds(p0, k)]
        for j in range(k):
            st_ref[slot, pl.ds(p0 + j, nrow, stride=SLAB_STRIDE), :] = tab_ref[sub[offs[j]]]


def _slab(st_ref, slot, r):
    return pltpu.bitcast(st_ref[slot, r * SLAB_STRIDE:r * SLAB_STRIDE + PICKS, :], jnp.bfloat16)


def _token_pipeline(n_tok, gather_part, compute_part, finish):
    g = PEER_GROUP

    def step(slot, t, prev_slot, t_prev):
        acc = None
        for part in range(GATHER_PARTS):
            gather_part(slot, t, part)
            if prev_slot is not None:
                acc = compute_part(prev_slot, t_prev, part, acc)
        if prev_slot is not None:
            finish(t_prev, acc)

    step(0, 0, None, None)
    for j in range(1, g):
        step(j, j, j - 1, j - 1)

    def group(gi, carry):
        t0 = gi * g
        step(0, t0, g - 1, t0 - 1)
        for j in range(1, g):
            step(j, t0 + j, j - 1, t0 + j - 1)
        return carry

    lax.fori_loop(1, n_tok // g, group, 0)
    acc = None
    for part in range(GATHER_PARTS):
        acc = compute_part(g - 1, n_tok - 1, part, acc)
    finish(n_tok - 1, acc)


def _expert_act_body(idx_ref, off_ref, h_ref, gate_ref, tab_ref, w_ref, st_ref):
    rid = lax.broadcasted_iota(jnp.int32, (SUBLANES, LANES), 0)
    lane = lax.broadcasted_iota(jnp.int32, (1, 2 * PICKS), 1)
    even = (lane % 2) == 0

    def compute_part(slot, t, r, acc):
        half = h_ref.shape[1] // 2
        hrow = h_ref[pl.ds(t, 1), :]
        xh = jnp.broadcast_to(hrow[:, r * LANES:(r + 1) * LANES], rid.shape)
        xl = jnp.broadcast_to(hrow[:, half + r * LANES:half + (r + 1) * LANES], rid.shape)
        res = lax.dot_general(_hi_lo_rows(xh, xl, rid), _slab(st_ref, slot, r), _NT,
                              preferred_element_type=jnp.float32)
        return res if acc is None else acc + res

    def finish(t, acc):
        v = jnp.where(even, acc[2:3] + acc[3:4], acc[0:1] + acc[1:2])
        act = v + jnp.where(even, pltpu.roll(v, 2 * PICKS - 1, axis=1), pltpu.roll(v, 1, axis=1))
        gl = 0.5 * act * (1.0 + lax.erf(act * (1.0 / math.sqrt(2.0))))
        w_ref[pl.ds(t, 1), :] = gate_ref[pl.ds(t, 1), :] * gl

    offs = [off_ref[j] for j in range(IDX_OFFSETS)]
    _token_pipeline(h_ref.shape[0], partial(_gather_part, idx_ref, offs, tab_ref, st_ref), compute_part, finish)


def _expert_out_body(idx_ref, off_ref, w_ref, tab_ref, x_ref, mod_ref, o_ref, st_ref):
    rid = lax.broadcasted_iota(jnp.int32, (SUBLANES, 2 * PICKS), 0)
    lane = lax.broadcasted_iota(jnp.int32, (SUBLANES, 2 * PICKS), 1)
    even = (lane % 2) == 0

    def compute_part(slot, t, r, acc):
        wrow = jnp.broadcast_to(w_ref[pl.ds(t, 1), :], rid.shape)
        lhs = _hi_lo_rows(jnp.where(even, 0.0, wrow), jnp.where(even, wrow, 0.0), rid)
        res = jnp.dot(lhs, _slab(st_ref, slot, r), preferred_element_type=jnp.float32)
        first, second = acc if acc is not None else ([], [])
        return first + [res[0:1] + res[1:2]], second + [res[2:3] + res[3:4]]

    g2 = mod_ref[0][5:6]

    def finish(t, acc):
        o_ref[pl.ds(t, 1), :] = x_ref[pl.ds(t, 1), :] + g2 * jnp.concatenate(acc[0] + acc[1], axis=1)

    offs = [off_ref[j] for j in range(IDX_OFFSETS)]
    _token_pipeline(o_ref.shape[0], partial(_gather_part, idx_ref, offs, tab_ref, st_ref), compute_part, finish)


def _expert_specs(t):
    nt = PEER_TOK_BLOCK
    nrow = D_MODEL // 2 // LANES
    assert nrow == GATHER_PARTS and nt % PEER_GROUP == 0 and nt // PEER_GROUP >= 2
    smem = pl.BlockSpec((nt, PICKS), lambda i: (i, 0), memory_space=pltpu.SMEM)
    vrow = pl.BlockSpec((nt, 2 * PICKS), lambda i: (i, 0))
    full = pl.BlockSpec((nt, D_MODEL), lambda i: (i, 0))
    table = pl.BlockSpec(memory_space=pltpu.VMEM)
    stage = pltpu.VMEM((PEER_GROUP, nrow * SLAB_STRIDE, LANES), jnp.int32)
    assert PICKS // GATHER_PARTS % IDX_OFFSETS == 0
    offs = pl.BlockSpec(memory_space=pltpu.SMEM)
    return nt, smem, offs, vrow, full, table, stage


def _expert_act(idx, h2, gate2, tab_u):
    t = idx.shape[0]
    nt, smem, offs, vrow, full, table, stage = _expert_specs(t)
    return pl.pallas_call(
        _expert_act_body,
        grid=(t // nt,),
        in_specs=[smem, offs, full, vrow, table],
        out_specs=vrow,
        out_shape=jax.ShapeDtypeStruct((t, 2 * PICKS), jnp.float32),
        scratch_shapes=[stage],
        compiler_params=_cparams(("arbitrary",)),
        name="peer_act",
    )(idx, jnp.arange(IDX_OFFSETS, dtype=jnp.int32), h2, gate2, tab_u)


def _expert_out(idx, w2, tab_v, xs, modl, blocks_per_batch, ctx_blocks):
    t = idx.shape[0]
    nt, smem, offs, vrow, full, table, stage = _expert_specs(t)
    mod = pl.BlockSpec((1, 6, D_MODEL), lambda i: (_mod_row(i, blocks_per_batch, ctx_blocks), 0, 0))
    return pl.pallas_call(
        _expert_out_body,
        grid=(t // nt,),
        in_specs=[smem, offs, vrow, table, full, mod],
        out_specs=full,
        out_shape=jax.ShapeDtypeStruct((t, D_MODEL), jnp.float32),
        scratch_shapes=[stage],
        compiler_params=_cparams(("arbitrary",)),
        name="peer_out",
    )(idx, jnp.arange(IDX_OFFSETS, dtype=jnp.int32), w2, tab_v, xs, modl)


def _final_norm_body(x_ref, g_ref, o_ref):
    xf = x_ref[...]
    o_ref[...] = xf * lax.rsqrt(jnp.mean(xf * xf, axis=-1, keepdims=True) + EPS) * g_ref[...]


def _final_norm(x2, g):
    rows, d = x2.shape
    tm = TOK_BLOCK
    return pl.pallas_call(
        _final_norm_body,
        grid=(rows // tm,),
        in_specs=[pl.BlockSpec((tm, d), lambda i: (i, 0)), pl.BlockSpec((1, d), lambda i: (0, 0))],
        out_specs=pl.BlockSpec((tm, d), lambda i: (i, 0)),
        out_shape=jax.ShapeDtypeStruct((rows, d), x2.dtype),
        compiler_params=_cparams(("arbitrary",)),
        name="final_norm",
    )(x2, g.reshape(1, d))


def _rope_tables(seq, lc, nb):
    rows = seq // GRID_W
    row_ids = jnp.repeat(jnp.arange(rows, dtype=jnp.float32), GRID_W)
    col_ids = jnp.tile(jnp.arange(GRID_W, dtype=jnp.float32), rows)
    n_freq = RET_HEAD_DIM // 4
    freqs = ROPE_BASE ** (-jnp.arange(n_freq, dtype=jnp.float32) / n_freq)
    ang_r, ang_c = row_ids[:, None] * freqs[None], col_ids[:, None] * freqs[None]
    cos = jnp.concatenate([jnp.cos(ang_r), jnp.cos(ang_r), jnp.cos(ang_c), jnp.cos(ang_c)], axis=-1)
    sin = jnp.concatenate([-jnp.sin(ang_r), jnp.sin(ang_r), -jnp.sin(ang_c), jnp.sin(ang_c)], axis=-1)
    cos = jnp.concatenate([jnp.ones((lc, RET_HEAD_DIM), jnp.float32), cos], axis=0)
    sin = jnp.concatenate([jnp.zeros((lc, RET_HEAD_DIM), jnp.float32), sin], axis=0)
    return jnp.tile(cos, (nb, 1)), jnp.tile(sin, (nb, 1))


def kernel(x, c, ctx, c_ctx, w_mod, b_mod, norm1_g, norm2_g, w_in, ssm_B_re, ssm_B_im,
           ssm_C_re, ssm_C_im, ssm_D, ssm_lam_re_f, ssm_lam_im_f, ssm_log_dt_f,
           ssm_lam_re_b, ssm_lam_im_b, ssm_log_dt_b, w_ssm_glu, ret_decay_f, ret_decay_b,
           w_ret_up, w_out, peer_w_q, peer_sub_keys, peer_u, peer_v, final_norm_g):
    nb, seq, d = x.shape
    lc = ctx.shape[1]
    depth = w_mod.shape[0]
    lb = lc + seq
    bf = jnp.bfloat16
    assert nb == 2 and d == D_MODEL and lc % TOK_BLOCK == 0 and seq % TOK_BLOCK == 0
    bpb, cbl = lb // TOK_BLOCK, lc // TOK_BLOCK
    bpb_r, cbl_r = lb // PEER_ROUTE_BLOCK, lc // PEER_ROUTE_BLOCK

    xs = jnp.concatenate([ctx, x], axis=1).reshape(nb * lb, d)
    cvec = jnp.zeros((SUBLANES, d), jnp.float32).at[:nb].set(c).at[nb].set(c_ctx)
    mod_all = _modulation(cvec, w_mod, b_mod)[:, :nb + 1].reshape(depth, nb + 1, 6, d)
    cos_t, sin_t = _rope_tables(seq, lc, nb)
    half = PEER_DKEY // 2

    for i in range(depth):
        modl = mod_all[i]
        s5_ops = _s5_operators(ssm_B_re[i], ssm_B_im[i], ssm_C_re[i], ssm_C_im[i], ssm_D[i],
                               ssm_lam_re_f[i], ssm_lam_im_f[i], ssm_log_dt_f[i],
                               ssm_lam_re_b[i], ssm_lam_im_b[i], ssm_log_dt_b[i])
        ret_c = _ret_consts(ret_decay_f[i], ret_decay_b[i])
        u, q, k, v, g, gs, gr = _in_proj(xs, modl, norm1_g[i].reshape(1, d), w_in[i].astype(bf),
                                         cos_t, sin_t, bpb, cbl)
        ys = _s5_mix(u, s5_ops, nb, lb, lc)
        yr = _retention(q, k, v, g, ret_c, nb, lb, lc)
        xs = _merge(xs, ys, yr, gs, gr, modl, w_ssm_glu[i].astype(bf), w_ret_up[i].astype(bf),
                    w_out[i].astype(bf), bpb, cbl)
        sk = peer_sub_keys[i].astype(bf)
        zeros = jnp.zeros_like(sk)
        keys_pad = jnp.stack([jnp.concatenate([sk[:, 0], zeros[:, 0]], axis=-1),
                              jnp.concatenate([zeros[:, 1], sk[:, 1]], axis=-1)], axis=1)
        h2, idx, gate = _route(xs, modl, norm2_g[i].reshape(1, d), peer_w_q[i].astype(bf),
                                   keys_pad, bpb_r, cbl_r)
        wts = _expert_act(idx, h2, jnp.repeat(gate, 2, axis=1), _pack_table(peer_u[i]))
        xs = _expert_out(idx, wts, _pack_table(peer_v[i]), xs, modl, lb // PEER_TOK_BLOCK, lc // PEER_TOK_BLOCK)

    lat = xs.reshape(nb, lb, d)[:, lc:].reshape(nb * seq, d)
    return _final_norm(lat, final_norm_g).reshape(nb, seq, d)
```

```python
import math
from functools import partial

import jax
import jax.numpy as jnp
from jax import lax
from jax.experimental import pallas as pl
from jax.experimental.pallas import tpu as pltpu

D_MODEL = 1024
GRID_W = 64
EPS = 1e-6
SSM_WIDTH = 512
SSM_GROUP = 16
SSM_GROUPS = SSM_WIDTH // SSM_GROUP
SSM_STATE = 64
RET_WIDTH = 512
RET_HEADS = 4
RET_HEAD_DIM = RET_WIDTH // RET_HEADS
ROPE_BASE = 10000.0
PEER_HEADS = 8
PEER_NKEYS = 128
PEER_DKEY = 128
PEER_TOPK = 16
IN_COLS = SSM_WIDTH + 4 * RET_WIDTH + 2 * D_MODEL

LANES = 128
SUBLANES = 8
VMEM_LIMIT = 56 * 1024 * 1024
TOK_BLOCK = 256
RET_CHUNK = 128
S5_CHUNK = 64
PEER_ROUTE_BLOCK = 128
PEER_TOK_BLOCK = 256
PEER_GROUP = 32
PICKS = PEER_HEADS * PEER_TOPK
SLAB_STRIDE = LANES + SUBLANES


def _cparams(sem=None):
    return pltpu.CompilerParams(dimension_semantics=sem, vmem_limit_bytes=VMEM_LIMIT)


def _mod_body(c_ref, w_ref, b_ref, o_ref):
    cv = c_ref[...]
    s = cv * jax.nn.sigmoid(cv)
    o_ref[0] = jnp.dot(s, w_ref[0], preferred_element_type=jnp.float32,
                       precision=lax.Precision.HIGHEST) + b_ref[0]


def _modulation(cvec, w_mod, b_mod):
    depth, d, d6 = w_mod.shape
    nt = d6 // d
    return pl.pallas_call(
        _mod_body,
        grid=(depth, nt),
        in_specs=[pl.BlockSpec((SUBLANES, d), lambda l, j: (0, 0)),
                  pl.BlockSpec((1, d, d), lambda l, j: (l, 0, j)),
                  pl.BlockSpec((1, 1, d), lambda l, j: (l, 0, j))],
        out_specs=pl.BlockSpec((1, SUBLANES, d), lambda l, j: (l, 0, j)),
        out_shape=jax.ShapeDtypeStruct((depth, SUBLANES, d6), jnp.float32),
        compiler_params=_cparams(("arbitrary", "arbitrary")),
        name="adaln_mod",
    )(cvec, w_mod, b_mod.reshape(depth, 1, d6))


def _mod_row(i, blocks_per_batch, ctx_blocks):
    b = i // blocks_per_batch
    j = i - b * blocks_per_batch
    return jnp.where(j < ctx_blocks, 2, b)


def _norm_mod(x, g, shift, scale):
    y = x * lax.rsqrt(jnp.mean(x * x, axis=-1, keepdims=True) + EPS)
    return (y * g) * (1.0 + scale) + shift


def _swap_halves(t):
    lane = lax.broadcasted_iota(jnp.int32, t.shape, 1)
    first = (lane % 64) < 32
    return jnp.where(first, pltpu.roll(t, 96, axis=1), pltpu.roll(t, 32, axis=1))


def _in_body(x_ref, mod_ref, g_ref, w_ref, cos_ref, sin_ref,
             u_ref, q_ref, k_ref, v_ref, gg_ref, gs_ref, gr_ref):
    m = mod_ref[0]
    h = _norm_mod(x_ref[...], g_ref[...], m[0:1], m[1:2]).astype(jnp.bfloat16)

    def proj(lo, hi):
        return jnp.dot(h, w_ref[:, lo:hi], preferred_element_type=jnp.float32)

    o = 0
    u_ref[...] = proj(o, o + SSM_WIDTH).astype(u_ref.dtype)
    o += SSM_WIDTH
    cos = cos_ref[...]
    sin = sin_ref[...]
    k_scale = RET_HEAD_DIM ** -0.5
    for dst, scl in ((q_ref, 1.0), (k_ref, k_scale)):
        t = proj(o, o + RET_WIDTH)
        for hd in range(RET_HEADS):
            th = t[:, hd * RET_HEAD_DIM:(hd + 1) * RET_HEAD_DIM]
            r = th * cos + _swap_halves(th) * sin
            if scl != 1.0:
                r = r * scl
            dst[:, hd * RET_HEAD_DIM:(hd + 1) * RET_HEAD_DIM] = r.astype(dst.dtype)
        o += RET_WIDTH
    v_ref[...] = proj(o, o + RET_WIDTH).astype(v_ref.dtype)
    o += RET_WIDTH
    gg_ref[...] = proj(o, o + RET_WIDTH).astype(gg_ref.dtype)
    o += RET_WIDTH
    gs_ref[...] = proj(o, o + D_MODEL).astype(gs_ref.dtype)
    o += D_MODEL
    gr_ref[...] = proj(o, o + D_MODEL).astype(gr_ref.dtype)


def _in_proj(xs, modl, g1, w_in_bf, cos_t, sin_t, blocks_per_batch, ctx_blocks):
    t, d = xs.shape
    tb = TOK_BLOCK
    row = lambda i: (i, 0)
    bf = jnp.bfloat16
    outs = [jax.ShapeDtypeStruct((t, SSM_WIDTH), bf)] + [jax.ShapeDtypeStruct((t, RET_WIDTH), bf)] * 4 \
        + [jax.ShapeDtypeStruct((t, D_MODEL), bf)] * 2
    return pl.pallas_call(
        _in_body,
        grid=(t // tb,),
        in_specs=[pl.BlockSpec((tb, d), row),
                  pl.BlockSpec((1, 6, d), lambda i: (_mod_row(i, blocks_per_batch, ctx_blocks), 0, 0)),
                  pl.BlockSpec((1, d), lambda i: (0, 0)),
                  pl.BlockSpec((d, IN_COLS), lambda i: (0, 0)),
                  pl.BlockSpec((tb, RET_HEAD_DIM), row),
                  pl.BlockSpec((tb, RET_HEAD_DIM), row)],
        out_specs=[pl.BlockSpec((tb, SSM_WIDTH), row)] + [pl.BlockSpec((tb, RET_WIDTH), row)] * 4
        + [pl.BlockSpec((tb, D_MODEL), row)] * 2,
        out_shape=outs,
        compiler_params=_cparams(("arbitrary",)),
        name="in_proj",
    )(xs, modl, g1, w_in_bf, cos_t, sin_t)


def _s5_operators(B_re, B_im, C_re, C_im, d_skip, lam_re_f, lam_im_f, log_dt_f,
                  lam_re_b, lam_im_b, log_dt_b):
    f32 = jnp.float32
    hp = lax.Precision.HIGHEST
    T = S5_CHUNK
    ks = jnp.arange(T + 1, dtype=f32)[:, None, None]

    def direction(lam_re, lam_im, log_dt):
        dt = jnp.exp(log_dt.astype(f32))[:, None]
        ar, ai = lam_re.astype(f32) * dt, lam_im.astype(f32) * dt
        mag = jnp.exp(ks * ar)
        pw_re, pw_im = mag * jnp.cos(ks * ai), mag * jnp.sin(ks * ai)
        x, y = pw_re[1] - 1.0, pw_im[1]
        den = lam_re * lam_re + lam_im * lam_im
        bf_re, bf_im = (x * lam_re + y * lam_im) / den, (y * lam_re - x * lam_im) / den
        bt_re = bf_re[..., None] * B_re - bf_im[..., None] * B_im
        bt_im = bf_re[..., None] * B_im + bf_im[..., None] * B_re
        cp_re = C_re[None] * pw_re[:, :, None, :] - C_im[None] * pw_im[:, :, None, :]
        cp_im = C_re[None] * pw_im[:, :, None, :] + C_im[None] * pw_re[:, :, None, :]
        taps = (jnp.einsum('kgjp,gpi->kgij', cp_re[:T], bt_re, precision=hp)
                - jnp.einsum('kgjp,gpi->kgij', cp_im[:T], bt_im, precision=hp))
        zw_re = pw_re[:T, :, :, None] * bt_re[None] - pw_im[:T, :, :, None] * bt_im[None]
        zw_im = pw_re[:T, :, :, None] * bt_im[None] + pw_im[:T, :, :, None] * bt_re[None]
        return taps, (zw_re, zw_im), (cp_re, cp_im), (pw_re[T], pw_im[T])

    taps_f, zw_f, cp_f, a_f = direction(lam_re_f, lam_im_f, log_dt_f)
    taps_b, zw_b, cp_b, a_b = direction(lam_re_b, lam_im_b, log_dt_b)
    G, I = SSM_GROUPS, SSM_GROUP
    dd = d_skip.astype(f32).reshape(G, I)
    center = taps_f[0] + taps_b[0] + dd[:, :, None] * jnp.eye(I, dtype=f32)[None]
    full = jnp.concatenate([taps_b[:0:-1], center[None], taps_f[1:]], axis=0)
    full = full.astype(jnp.bfloat16)
    s_i = jnp.arange(T)
    shift = ((s_i[None, :] - s_i[:, None]) + T - 1)[None] == jnp.arange(2 * T - 1)[:, None, None]
    m_op = jnp.einsum('dst,dgij->gsitj', shift.astype(jnp.bfloat16), full,
                      preferred_element_type=jnp.bfloat16).reshape(G, T * I, T * I)

    def zcols(zw, flip):
        re, im = zw
        if flip:
            re, im = re[::-1], im[::-1]
        f = lambda a: a.transpose(1, 0, 3, 2).reshape(G, T * I, SSM_STATE)
        return [f(re), f(im)]

    v_op = jnp.concatenate(zcols(zw_f, True) + zcols(zw_b, False), axis=-1)

    def wrows(cp, idx):
        re, im = cp
        f = lambda a: a[idx].transpose(1, 3, 0, 2).reshape(G, SSM_STATE, T * I)
        return [f(re), -f(im)]

    w_op = jnp.concatenate(wrows(cp_f, jnp.arange(1, T + 1)) + wrows(cp_b, T - jnp.arange(T)), axis=1)
    a1 = jnp.stack([a_f[0], a_f[0], a_b[0], a_b[0]]).reshape(4, G * SSM_STATE)
    a2 = jnp.stack([-a_f[1], a_f[1], -a_b[1], a_b[1]]).reshape(4, G * SSM_STATE)
    return (m_op, v_op.astype(jnp.bfloat16)), w_op.astype(jnp.bfloat16), a1, a2


def _s5_intra_body(u_ref, m_ref, v_ref, y_ref, z_ref):
    u = u_ref[0]
    y_ref[0] = jnp.dot(u, m_ref[0], preferred_element_type=jnp.float32)
    z_ref[0] = jnp.dot(u, v_ref[0], preferred_element_type=jnp.float32)


def _s5_intra(ug, mv):
    m_op, v_op = mv
    g, m, kdim = ug.shape
    n, nz = m_op.shape[-1], v_op.shape[-1]
    return pl.pallas_call(
        _s5_intra_body,
        grid=(g,),
        in_specs=[pl.BlockSpec((1, m, kdim), lambda i: (i, 0, 0)),
                  pl.BlockSpec((1, kdim, n), lambda i: (i, 0, 0)),
                  pl.BlockSpec((1, kdim, nz), lambda i: (i, 0, 0))],
        out_specs=[pl.BlockSpec((1, m, n), lambda i: (i, 0, 0)),
                   pl.BlockSpec((1, m, nz), lambda i: (i, 0, 0))],
        out_shape=[jax.ShapeDtypeStruct((g, m, n), jnp.float32),
                   jax.ShapeDtypeStruct((g, m, nz), jnp.float32)],
        compiler_params=_cparams(("arbitrary",)),
        name="s5_intra",
    )(ug, m_op, v_op)


def _s5_scan_body(z_ref, a1_ref, a2_ref, s_ref):
    steps = z_ref.shape[0]
    a1 = a1_ref[...]
    a2 = a2_ref[...]
    row = lax.broadcasted_iota(jnp.int32, a1.shape, 0)
    even = (row % 2) == 0

    def step(k, s):
        s_ref[k] = s
        partner = jnp.where(even, pltpu.roll(s, SUBLANES - 1, axis=0), pltpu.roll(s, 1, axis=0))
        return a1 * s + a2 * partner + z_ref[k]

    lax.fori_loop(0, steps, step, jnp.zeros(a1.shape, jnp.float32))


def _s5_scan(z8, a1, a2):
    steps, r, n = z8.shape
    cb = 512
    return pl.pallas_call(
        _s5_scan_body,
        grid=(n // cb,),
        in_specs=[pl.BlockSpec((steps, r, cb), lambda i: (0, 0, i)),
                  pl.BlockSpec((r, cb), lambda i: (0, i)),
                  pl.BlockSpec((r, cb), lambda i: (0, i))],
        out_specs=pl.BlockSpec((steps, r, cb), lambda i: (0, 0, i)),
        out_shape=jax.ShapeDtypeStruct((steps, r, n), jnp.float32),
        compiler_params=_cparams(("arbitrary",)),
        name="s5_scan",
    )(z8, a1, a2)


def _s5_out_body(y_ref, s_ref, w_ref, o_ref):
    o_ref[0] = (y_ref[0] + jnp.dot(s_ref[0], w_ref[0], preferred_element_type=jnp.float32)).astype(o_ref.dtype)


def _s5_out(yi, sg, w_op):
    g, m, n = yi.shape
    ks = sg.shape[-1]
    return pl.pallas_call(
        _s5_out_body,
        grid=(g,),
        in_specs=[pl.BlockSpec((1, m, n), lambda i: (i, 0, 0)),
                  pl.BlockSpec((1, m, ks), lambda i: (i, 0, 0)),
                  pl.BlockSpec((1, ks, n), lambda i: (i, 0, 0))],
        out_specs=pl.BlockSpec((1, m, n), lambda i: (i, 0, 0)),
        out_shape=jax.ShapeDtypeStruct((g, m, n), jnp.bfloat16),
        compiler_params=_cparams(("arbitrary",)),
        name="s5_out",
    )(yi, sg, w_op)


def _s5_mix(u, ops, nb, lb, lc):
    mv, w_op, a1, a2 = ops
    G, I, P, T = SSM_GROUPS, SSM_GROUP, SSM_STATE, S5_CHUNK
    nch = lb // T
    cch = lc // T
    ug = u.reshape(nb * nch, T, G, I).transpose(2, 0, 1, 3).reshape(G, nb * nch, T * I)
    yi, z = _s5_intra(ug, mv)
    order_b = jnp.concatenate([jnp.arange(cch - 1, -1, -1), jnp.arange(nch - 1, cch - 1, -1)])
    z6 = z.reshape(G, nb, nch, 2, 2, P)
    zf = z6[:, :, :, 0]
    zb = z6[:, :, order_b, 1]
    z8 = jnp.stack([zf, zb], axis=3)
    z8 = z8.transpose(2, 1, 3, 4, 0, 5).reshape(nch, nb * 4, G * P)
    rows = nb * 4
    pad = (-rows) % SUBLANES
    reps = (rows + pad) // 4
    if pad:
        z8 = jnp.pad(z8, ((0, 0), (0, pad), (0, 0)))
    s8 = _s5_scan(z8, jnp.tile(a1, (reps, 1)), jnp.tile(a2, (reps, 1)))[:, :rows]
    s6 = s8.reshape(nch, nb, 2, 2, G, P).transpose(4, 1, 0, 2, 3, 5)
    inv_b = jnp.argsort(order_b)
    sf = s6[:, :, :, 0]
    sb = s6[:, :, inv_b, 1]
    sg = jnp.stack([sf, sb], axis=3).reshape(G, nb * nch, 4 * P).astype(jnp.bfloat16)
    y = _s5_out(yi, sg, w_op)
    return y.reshape(G, nb * nch, T, I).transpose(1, 2, 0, 3).reshape(nb * lb, G * I)


def _ret_consts(ret_decay_f, ret_decay_b):
    f32 = jnp.float32
    C = RET_CHUNK
    lg_f = -jnp.exp(ret_decay_f.astype(f32))[:, None, None]
    lg_b = -jnp.exp(ret_decay_b.astype(f32))[:, None, None]
    pos = jnp.arange(C, dtype=f32)
    diff = pos[:, None] - pos[None, :]
    dmat = jnp.where(diff >= 0, jnp.exp(jnp.where(diff >= 0, diff, 0.0)[None] * lg_f),
                     jnp.exp(jnp.where(diff < 0, -diff, 0.0)[None] * lg_b))
    col = lambda e: jnp.broadcast_to(jnp.exp(e), (RET_HEADS, C, RET_HEAD_DIM))
    p1 = pos[None, :, None]
    qdec_f = col((p1 + 1.0) * lg_f)
    kdec_f = col((C - 1.0 - p1) * lg_f)
    qdec_b = col((C - p1) * lg_b)
    kdec_b = col(p1 * lg_b)
    cd_f = col(jnp.full_like(p1, C) * lg_f)
    cd_b = col(jnp.full_like(p1, C) * lg_b)
    return dmat, qdec_f, kdec_f, cd_f, qdec_b, kdec_b, cd_b


def _ret_state_update(s_ref, b, hd, kh, vh, kdec, cd):
    kd = (kh.astype(jnp.float32) * kdec).astype(jnp.bfloat16)
    inc = lax.dot_general(kd, vh, (((0,), (0,)), ((), ())), preferred_element_type=jnp.float32)
    s_ref[b, hd] = cd * s_ref[b, hd] + inc


def _ret_fwd_body(q_ref, k_ref, v_ref, dm_ref, qd_ref, kd_ref, cd_ref, o_ref, s_ref):
    @pl.when(pl.program_id(0) == 0)
    def _():
        s_ref[...] = jnp.zeros(s_ref.shape, s_ref.dtype)

    for b in range(q_ref.shape[0]):
        for hd in range(RET_HEADS):
            sl = slice(hd * RET_HEAD_DIM, (hd + 1) * RET_HEAD_DIM)
            qh, kh, vh = q_ref[b, :, sl], k_ref[b, :, sl], v_ref[b, :, sl]
            sc = lax.dot_general(qh, kh, (((1,), (1,)), ((), ())), preferred_element_type=jnp.float32)
            p = (sc * dm_ref[hd]).astype(jnp.bfloat16)
            o = jnp.dot(p, vh, preferred_element_type=jnp.float32)
            cross = jnp.dot(qh, s_ref[b, hd].astype(jnp.bfloat16), preferred_element_type=jnp.float32)
            o_ref[b, :, sl] = o + cross * qd_ref[hd]
            _ret_state_update(s_ref, b, hd, kh, vh, kd_ref[hd], cd_ref[hd])


def _ret_bwd_body(q_ref, k_ref, v_ref, of_ref, g_ref, qd_ref, kd_ref, cd_ref, y_ref, s_ref):
    @pl.when(pl.program_id(0) == 0)
    def _():
        s_ref[...] = jnp.zeros(s_ref.shape, s_ref.dtype)

    for b in range(q_ref.shape[0]):
        for hd in range(RET_HEADS):
            sl = slice(hd * RET_HEAD_DIM, (hd + 1) * RET_HEAD_DIM)
            qh, kh, vh = q_ref[b, :, sl], k_ref[b, :, sl], v_ref[b, :, sl]
            cross = jnp.dot(qh, s_ref[b, hd].astype(jnp.bfloat16), preferred_element_type=jnp.float32)
            o = of_ref[b, :, sl] + cross * qd_ref[hd]
            o = o * lax.rsqrt(jnp.mean(o * o, axis=-1, keepdims=True) + EPS)
            gv = g_ref[b, :, sl].astype(jnp.float32)
            y_ref[b, :, sl] = (o * (gv * jax.nn.sigmoid(gv))).astype(y_ref.dtype)
            _ret_state_update(s_ref, b, hd, kh, vh, kd_ref[hd], cd_ref[hd])


def _retention(q, k, v, g, consts, nb, lb, lc):
    dmat, qdec_f, kdec_f, cd_f, qdec_b, kdec_b, cd_b = consts
    C, W = RET_CHUNK, RET_WIDTH
    nblk, cblk = lb // C, lc // C
    r3 = lambda a: a.reshape(nb, lb, W)
    fwd_idx = lambda s: (0, s, 0)
    bwd_idx = lambda s: (0, jnp.where(s < cblk, cblk - 1 - s, nblk - 1 + cblk - s), 0)
    cst = pl.BlockSpec((RET_HEADS, C, RET_HEAD_DIM), lambda s: (0, 0, 0))
    state = pltpu.VMEM((nb, RET_HEADS, RET_HEAD_DIM, RET_HEAD_DIM), jnp.float32)
    o_f = pl.pallas_call(
        _ret_fwd_body,
        grid=(nblk,),
        in_specs=[pl.BlockSpec((nb, C, W), fwd_idx)] * 3 + [cst] * 4,
        out_specs=pl.BlockSpec((nb, C, W), fwd_idx),
        out_shape=jax.ShapeDtypeStruct((nb, lb, W), jnp.float32),
        scratch_shapes=[state],
        compiler_params=_cparams(("arbitrary",)),
        name="ret_fwd",
    )(r3(q), r3(k), r3(v), dmat, qdec_f, kdec_f, cd_f)
    y = pl.pallas_call(
        _ret_bwd_body,
        grid=(nblk,),
        in_specs=[pl.BlockSpec((nb, C, W), bwd_idx)] * 5 + [cst] * 3,
        out_specs=pl.BlockSpec((nb, C, W), bwd_idx),
        out_shape=jax.ShapeDtypeStruct((nb, lb, W), jnp.bfloat16),
        scratch_shapes=[state],
        compiler_params=_cparams(("arbitrary",)),
        name="ret_bwd",
    )(r3(q), r3(k), r3(v), o_f, r3(g), qdec_b, kdec_b, cd_b)
    return y.reshape(nb * lb, W)


def _gelu_tanh(x):
    return 0.5 * x * (1.0 + jnp.tanh(math.sqrt(2.0 / math.pi) * (x + 0.044715 * x * x * x)))


def _merge_body(x_ref, ys_ref, yr_ref, gs_ref, gr_ref, mod_ref, wg_ref, wr_ref, wo_ref, o_ref):
    f32 = jnp.float32
    a_in = _gelu_tanh(ys_ref[...].astype(f32)).astype(jnp.bfloat16)
    ab = jnp.dot(a_in, wg_ref[...], preferred_element_type=f32)
    ys = ab[:, :D_MODEL] * jax.nn.sigmoid(ab[:, D_MODEL:])
    yr = jnp.dot(yr_ref[...], wr_ref[...], preferred_element_type=f32)
    m = jax.nn.sigmoid(gs_ref[...].astype(f32)) * ys + jax.nn.sigmoid(gr_ref[...].astype(f32)) * yr
    y = jnp.dot(m.astype(jnp.bfloat16), wo_ref[...], preferred_element_type=f32)
    o_ref[...] = x_ref[...] + mod_ref[0][2:3] * y


def _merge(xs, ys, yr, gs, gr, modl, wg, wr, wo, blocks_per_batch, ctx_blocks):
    t, d = xs.shape
    tb = TOK_BLOCK
    row = lambda i: (i, 0)
    full = lambda a: pl.BlockSpec(a.shape, lambda i: (0, 0))
    return pl.pallas_call(
        _merge_body,
        grid=(t // tb,),
        in_specs=[pl.BlockSpec((tb, d), row), pl.BlockSpec((tb, SSM_WIDTH), row),
                  pl.BlockSpec((tb, RET_WIDTH), row), pl.BlockSpec((tb, d), row), pl.BlockSpec((tb, d), row),
                  pl.BlockSpec((1, 6, d), lambda i: (_mod_row(i, blocks_per_batch, ctx_blocks), 0, 0)),
                  full(wg), full(wr), full(wo)],
        out_specs=pl.BlockSpec((tb, d), row),
        out_shape=jax.ShapeDtypeStruct((t, d), jnp.float32),
        compiler_params=_cparams(("arbitrary",)),
        name="merge_out",
    )(xs, ys, yr, gs, gr, modl, wg, wr, wo)


def _top16_rows(vals, payload=None):
    n = vals.shape[0]
    rid = lax.broadcasted_iota(jnp.int32, vals.shape, 0).astype(jnp.float32)
    top_v, top_p = [], []
    for _ in range(PEER_TOPK):
        m = jnp.max(vals, axis=0, keepdims=True)
        first = jnp.min(jnp.where(vals == m, rid, float(n)), axis=0, keepdims=True)
        hit = rid == first
        top_v.append(m)
        top_p.append(first if payload is None else
                     jnp.max(jnp.where(hit, payload, -1.0), axis=0, keepdims=True))
        vals = jnp.where(hit, -jnp.inf, vals)
    return jnp.concatenate(top_v, axis=0), jnp.concatenate(top_p, axis=0)


def _pair_candidates(s1, i1, s2, i2):
    k = PEER_TOPK
    bid = lax.broadcasted_iota(jnp.int32, (SUBLANES, s1.shape[1]), 0)
    cs, ce = [s1[0:1] + s2], [i1[0:1] * float(PEER_NKEYS) + i2]
    for a in range(1, SUBLANES):
        keep = bid < (k // (a + 1))
        cs.append(jnp.where(keep, s1[a:a + 1] + s2[0:SUBLANES], -jnp.inf))
        ce.append(i1[a:a + 1] * float(PEER_NKEYS) + i2[0:SUBLANES])
    cs.append(s1[SUBLANES:k] + s2[0:1])
    ce.append(i1[SUBLANES:k] * float(PEER_NKEYS) + i2[0:1])
    return jnp.concatenate(cs, axis=0), jnp.concatenate(ce, axis=0)


def _route_body(x_ref, mod_ref, g_ref, wq_ref, keys_ref, h_ref, idx_ref, gate_ref):
    m = mod_ref[0]
    h2 = _norm_mod(x_ref[...], g_ref[...], m[3:4], m[4:5])
    h_ref[...] = h2
    q = jnp.dot(h2.astype(jnp.bfloat16), wq_ref[...], preferred_element_type=jnp.float32)
    idx_rows, gate_rows = [], []
    for hd in range(PEER_HEADS):
        qh = q[:, hd * PEER_DKEY:(hd + 1) * PEER_DKEY].astype(jnp.bfloat16)
        tops = []
        for s in range(2):
            st = lax.dot_general(keys_ref[hd, s], qh, (((1,), (1,)), ((), ())),
                                 preferred_element_type=jnp.float32)
            tops.append(_top16_rows(st))
        (s1, i1), (s2, i2) = tops
        best_s, best_e = _top16_rows(*_pair_candidates(s1, i1, s2, i2))
        ex = jnp.exp(best_s - best_s[0:1])
        gate_rows.append(ex / jnp.sum(ex, axis=0, keepdims=True))
        idx_rows.append(best_e)
    idx_ref[...] = jnp.concatenate(idx_rows, axis=0).T.astype(jnp.int32)
    gate_ref[...] = jnp.concatenate(gate_rows, axis=0).T


def _route(xs, modl, g2, wq_bf, keys_pad, blocks_per_batch, ctx_blocks):
    t, d = xs.shape
    tb = PEER_ROUTE_BLOCK
    row = lambda i: (i, 0)
    return pl.pallas_call(
        _route_body,
        grid=(t // tb,),
        in_specs=[pl.BlockSpec((tb, d), row),
                  pl.BlockSpec((1, 6, d), lambda i: (_mod_row(i, blocks_per_batch, ctx_blocks), 0, 0)),
                  pl.BlockSpec((1, d), lambda i: (0, 0)),
                  pl.BlockSpec((d, PEER_HEADS * PEER_DKEY), lambda i: (0, 0)),
                  pl.BlockSpec(keys_pad.shape, lambda i: (0, 0, 0, 0))],
        out_specs=[pl.BlockSpec((tb, d), row), pl.BlockSpec((tb, PICKS), row), pl.BlockSpec((tb, PICKS), row)],
        out_shape=[jax.ShapeDtypeStruct((t, d), jnp.float32),
                   jax.ShapeDtypeStruct((t, PICKS), jnp.int32), jax.ShapeDtypeStruct((t, PICKS), jnp.float32)],
        compiler_params=_cparams(("arbitrary",)),
        name="peer_route",
    )(xs, modl, g2, wq_bf, keys_pad)


def _pack_table(tab):
    n, d = tab.shape
    halves = tab.reshape(n, 2, d // 2 // LANES, LANES)
    bits = lax.bitcast_convert_type(halves.astype(jnp.bfloat16), jnp.uint16).astype(jnp.uint32)
    return lax.bitcast_convert_type((bits[:, 0] << 16) | bits[:, 1], jnp.int32)


_NT = (((1,), (1,)), ((), ()))
GATHER_PARTS = 4
IDX_OFFSETS = 8


def _hi_lo_rows(first, second, rid):
    f_hi = first.astype(jnp.bfloat16).astype(jnp.float32)
    s_hi = second.astype(jnp.bfloat16).astype(jnp.float32)
    v = jnp.where(rid == 0, f_hi, jnp.where(rid == 1, first - f_hi,
                  jnp.where(rid == 2, s_hi, jnp.where(rid == 3, second - s_hi, 0.0))))
    return v.astype(jnp.bfloat16)


def _gather_part(idx_ref, offs, tab_ref, st_ref, slot, t, part):
    n = PICKS // GATHER_PARTS
    nrow = tab_ref.shape[1]
    k = len(offs)
    for p0 in range(part * n, (part + 1) * n, k):
        sub = idx_ref.at[t, pl.ds(p0, k)]
        for j in range(k):
            st_ref[slot, pl.ds(p0 + j, nrow, stride=SLAB_STRIDE), :] = tab_ref[sub[offs[j]]]


def _slab(st_ref, slot, r):
    return pltpu.bitcast(st_ref[slot, r * SLAB_STRIDE:r * SLAB_STRIDE + PICKS, :], jnp.bfloat16)


def _token_pipeline(n_tok, gather_part, compute_part, finish):
    g = PEER_GROUP

    def step(slot, t, prev_slot, t_prev):
        acc = None
        for part in range(GATHER_PARTS):
            gather_part(slot, t, part)
            if prev_slot is not None:
                acc = compute_part(prev_slot, t_prev, part, acc)
        if prev_slot is not None:
            finish(t_prev, acc)

    step(0, 0, None, None)
    for j in range(1, g):
        step(j, j, j - 1, j - 1)

    def group(gi, carry):
        t0 = gi * g
        step(0, t0, g - 1, t0 - 1)
        for j in range(1, g):
            step(j, t0 + j, j - 1, t0 + j - 1)
        return carry

    lax.fori_loop(1, n_tok // g, group, 0)
    acc = None
    for part in range(GATHER_PARTS):
        acc = compute_part(g - 1, n_tok - 1, part, acc)
    finish(n_tok - 1, acc)


def _expert_act_body(idx_ref, off_ref, h_ref, gate_ref, tab_ref, w_ref, st_ref):
    rid = lax.broadcasted_iota(jnp.int32, (SUBLANES, LANES), 0)
    lane = lax.broadcasted_iota(jnp.int32, (1, 2 * PICKS), 1)
    even = (lane % 2) == 0

    def compute_part(slot, t, r, acc):
        half = h_ref.shape[1] // 2
        hrow = h_ref[pl.ds(t, 1), :]
        xh = jnp.broadcast_to(hrow[:, r * LANES:(r + 1) * LANES], rid.shape)
        xl = jnp.broadcast_to(hrow[:, half + r * LANES:half + (r + 1) * LANES], rid.shape)
        res = lax.dot_general(_hi_lo_rows(xh, xl, rid), _slab(st_ref, slot, r), _NT,
                              preferred_element_type=jnp.float32)
        return res if acc is None else acc + res

    def finish(t, acc):
        v = jnp.where(even, acc[2:3] + acc[3:4], acc[0:1] + acc[1:2])
        act = v + jnp.where(even, pltpu.roll(v, 2 * PICKS - 1, axis=1), pltpu.roll(v, 1, axis=1))
        gl = 0.5 * act * (1.0 + lax.erf(act * (1.0 / math.sqrt(2.0))))
        w_ref[pl.ds(t, 1), :] = gate_ref[pl.ds(t, 1), :] * gl

    offs = [off_ref[j] for j in range(IDX_OFFSETS)]
    _token_pipeline(h_ref.shape[0], partial(_gather_part, idx_ref, offs, tab_ref, st_ref), compute_part, finish)


def _expert_out_body(idx_ref, off_ref, w_ref, tab_ref, x_ref, mod_ref, o_ref, st_ref):
    rid = lax.broadcasted_iota(jnp.int32, (SUBLANES, 2 * PICKS), 0)
    lane = lax.broadcasted_iota(jnp.int32, (SUBLANES, 2 * PICKS), 1)
    even = (lane % 2) == 0

    def compute_part(slot, t, r, acc):
        wrow = jnp.broadcast_to(w_ref[pl.ds(t, 1), :], rid.shape)
        lhs = _hi_lo_rows(jnp.where(even, 0.0, wrow), jnp.where(even, wrow, 0.0), rid)
        res = jnp.dot(lhs, _slab(st_ref, slot, r), preferred_element_type=jnp.float32)
        first, second = acc if acc is not None else ([], [])
        return first + [res[0:1] + res[1:2]], second + [res[2:3] + res[3:4]]

    g2 = mod_ref[0][5:6]

    def finish(t, acc):
        o_ref[pl.ds(t, 1), :] = x_ref[pl.ds(t, 1), :] + g2 * jnp.concatenate(acc[0] + acc[1], axis=1)

    offs = [off_ref[j] for j in range(IDX_OFFSETS)]
    _token_pipeline(o_ref.shape[0], partial(_gather_part, idx_ref, offs, tab_ref, st_ref), compute_part, finish)


def _expert_specs(t):
    nt = PEER_TOK_BLOCK
    nrow = D_MODEL // 2 // LANES
    assert nrow == GATHER_PARTS and nt % PEER_GROUP == 0 and nt // PEER_GROUP >= 2
    smem = pl.BlockSpec((nt, PICKS), lambda i: (i, 0), memory_space=pltpu.SMEM)
    vrow = pl.BlockSpec((nt, 2 * PICKS), lambda i: (i, 0))
    full = pl.BlockSpec((nt, D_MODEL), lambda i: (i, 0))
    table = pl.BlockSpec(memory_space=pltpu.VMEM)
    stage = pltpu.VMEM((PEER_GROUP, nrow * SLAB_STRIDE, LANES), jnp.int32)
    assert PICKS // GATHER_PARTS % IDX_OFFSETS == 0
    offs = pl.BlockSpec(memory_space=pltpu.SMEM)
    return nt, smem, offs, vrow, full, table, stage


def _expert_act(idx, h2, gate2, tab_u):
    t = idx.shape[0]
    nt, smem, offs, vrow, full, table, stage = _expert_specs(t)
    return pl.pallas_call(
        _expert_act_body,
        grid=(t // nt,),
        in_specs=[smem, offs, full, vrow, table],
        out_specs=vrow,
        out_shape=jax.ShapeDtypeStruct((t, 2 * PICKS), jnp.float32),
        scratch_shapes=[stage],
        compiler_params=_cparams(("arbitrary",)),
        name="peer_act",
    )(idx, jnp.arange(IDX_OFFSETS, dtype=jnp.int32), h2, gate2, tab_u)


def _expert_out(idx, w2, tab_v, xs, modl, blocks_per_batch, ctx_blocks):
    t = idx.shape[0]
    nt, smem, offs, vrow, full, table, stage = _expert_specs(t)
    mod = pl.BlockSpec((1, 6, D_MODEL), lambda i: (_mod_row(i, blocks_per_batch, ctx_blocks), 0, 0))
    return pl.pallas_call(
        _expert_out_body,
        grid=(t // nt,),
        in_specs=[smem, offs, vrow, table, full, mod],
        out_specs=full,
        out_shape=jax.ShapeDtypeStruct((t, D_MODEL), jnp.float32),
        scratch_shapes=[stage],
        compiler_params=_cparams(("arbitrary",)),
        name="peer_out",
    )(idx, jnp.arange(IDX_OFFSETS, dtype=jnp.int32), w2, tab_v, xs, modl)


def _final_norm_body(x_ref, g_ref, o_ref):
    xf = x_ref[...]
    o_ref[...] = xf * lax.rsqrt(jnp.mean(xf * xf, axis=-1, keepdims=True) + EPS) * g_ref[...]


def _final_norm(x2, g):
    rows, d = x2.shape
    tm = TOK_BLOCK
    return pl.pallas_call(
        _final_norm_body,
        grid=(rows // tm,),
        in_specs=[pl.BlockSpec((tm, d), lambda i: (i, 0)), pl.BlockSpec((1, d), lambda i: (0, 0))],
        out_specs=pl.BlockSpec((tm, d), lambda i: (i, 0)),
        out_shape=jax.ShapeDtypeStruct((rows, d), x2.dtype),
        compiler_params=_cparams(("arbitrary",)),
        name="final_norm",
    )(x2, g.reshape(1, d))


def _rope_tables(seq, lc, nb):
    rows = seq // GRID_W
    row_ids = jnp.repeat(jnp.arange(rows, dtype=jnp.float32), GRID_W)
    col_ids = jnp.tile(jnp.arange(GRID_W, dtype=jnp.float32), rows)
    n_freq = RET_HEAD_DIM // 4
    freqs = ROPE_BASE ** (-jnp.arange(n_freq, dtype=jnp.float32) / n_freq)
    ang_r, ang_c = row_ids[:, None] * freqs[None], col_ids[:, None] * freqs[None]
    cos = jnp.concatenate([jnp.cos(ang_r), jnp.cos(ang_r), jnp.cos(ang_c), jnp.cos(ang_c)], axis=-1)
    sin = jnp.concatenate([-jnp.sin(ang_r), jnp.sin(ang_r), -jnp.sin(ang_c), jnp.sin(ang_c)], axis=-1)
    cos = jnp.concatenate([jnp.ones((lc, RET_HEAD_DIM), jnp.float32), cos], axis=0)
    sin = jnp.concatenate([jnp.zeros((lc, RET_HEAD_DIM), jnp.float32), sin], axis=0)
    return jnp.tile(cos, (nb, 1)), jnp.tile(sin, (nb, 1))


def kernel(x, c, ctx, c_ctx, w_mod, b_mod, norm1_g, norm2_g, w_in, ssm_B_re, ssm_B_im,
           ssm_C_re, ssm_C_im, ssm_D, ssm_lam_re_f, ssm_lam_im_f, ssm_log_dt_f,
           ssm_lam_re_b, ssm_lam_im_b, ssm_log_dt_b, w_ssm_glu, ret_decay_f, ret_decay_b,
           w_ret_up, w_out, peer_w_q, peer_sub_keys, peer_u, peer_v, final_norm_g):
    nb, seq, d = x.shape
    lc = ctx.shape[1]
    depth = w_mod.shape[0]
    lb = lc + seq
    bf = jnp.bfloat16
    assert nb == 2 and d == D_MODEL and lc % TOK_BLOCK == 0 and seq % TOK_BLOCK == 0
    bpb, cbl = lb // TOK_BLOCK, lc // TOK_BLOCK
    bpb_r, cbl_r = lb // PEER_ROUTE_BLOCK, lc // PEER_ROUTE_BLOCK

    xs = jnp.concatenate([ctx, x], axis=1).reshape(nb * lb, d)
    cvec = jnp.zeros((SUBLANES, d), jnp.float32).at[:nb].set(c).at[nb].set(c_ctx)
    mod_all = _modulation(cvec, w_mod, b_mod)[:, :nb + 1].reshape(depth, nb + 1, 6, d)
    cos_t, sin_t = _rope_tables(seq, lc, nb)
    half = PEER_DKEY // 2

    for i in range(depth):
        modl = mod_all[i]
        s5_ops = _s5_operators(ssm_B_re[i], ssm_B_im[i], ssm_C_re[i], ssm_C_im[i], ssm_D[i],
                               ssm_lam_re_f[i], ssm_lam_im_f[i], ssm_log_dt_f[i],
                               ssm_lam_re_b[i], ssm_lam_im_b[i], ssm_log_dt_b[i])
        ret_c = _ret_consts(ret_decay_f[i], ret_decay_b[i])
        u, q, k, v, g, gs, gr = _in_proj(xs, modl, norm1_g[i].reshape(1, d), w_in[i].astype(bf),
                                         cos_t, sin_t, bpb, cbl)
        ys = _s5_mix(u, s5_ops, nb, lb, lc)
        yr = _retention(q, k, v, g, ret_c, nb, lb, lc)
        xs = _merge(xs, ys, yr, gs, gr, modl, w_ssm_glu[i].astype(bf), w_ret_up[i].astype(bf),
                    w_out[i].astype(bf), bpb, cbl)
        sk = peer_sub_keys[i].astype(bf)
        zeros = jnp.zeros_like(sk)
        keys_pad = jnp.stack([jnp.concatenate([sk[:, 0], zeros[:, 0]], axis=-1),
                              jnp.concatenate([zeros[:, 1], sk[:, 1]], axis=-1)], axis=1)
        h2, idx, gate = _route(xs, modl, norm2_g[i].reshape(1, d), peer_w_q[i].astype(bf),
                                   keys_pad, bpb_r, cbl_r)
        wts = _expert_act(idx, h2, jnp.repeat(gate, 2, axis=1), _pack_table(peer_u[i]))
        xs = _expert_out(idx, wts, _pack_table(peer_v[i]), xs, modl, lb // PEER_TOK_BLOCK, lc // PEER_TOK_BLOCK)

    lat = xs.reshape(nb, lb, d)[:, lc:].reshape(nb * seq, d)
    return _final_norm(lat, final_norm_g).reshape(nb, seq, d)
```

```python
import math
from functools import partial

import jax
import jax.numpy as jnp
from jax import lax
from jax.experimental import pallas as pl
from jax.experimental.pallas import tpu as pltpu

D_MODEL = 1024
GRID_W = 64
EPS = 1e-6
SSM_WIDTH = 512
SSM_GROUP = 16
SSM_GROUPS = SSM_WIDTH // SSM_GROUP
SSM_STATE = 64
RET_WIDTH = 512
RET_HEADS = 4
RET_HEAD_DIM = RET_WIDTH // RET_HEADS
ROPE_BASE = 10000.0
PEER_HEADS = 8
PEER_NKEYS = 128
PEER_DKEY = 128
PEER_TOPK = 16
IN_COLS = SSM_WIDTH + 4 * RET_WIDTH + 2 * D_MODEL

LANES = 128
SUBLANES = 8
VMEM_LIMIT = 56 * 1024 * 1024
TOK_BLOCK = 256
RET_CHUNK = 128
S5_CHUNK = 64
PEER_ROUTE_BLOCK = 128
PEER_TOK_BLOCK = 256
PEER_GROUP = 32
PICKS = PEER_HEADS * PEER_TOPK
SLAB_STRIDE = LANES + 4


def _cparams(sem=None):
    return pltpu.CompilerParams(dimension_semantics=sem, vmem_limit_bytes=VMEM_LIMIT)


def _mod_body(c_ref, w_ref, b_ref, o_ref):
    cv = c_ref[...]
    s = cv * jax.nn.sigmoid(cv)
    o_ref[0] = jnp.dot(s, w_ref[0], preferred_element_type=jnp.float32,
                       precision=lax.Precision.HIGHEST) + b_ref[0]


def _modulation(cvec, w_mod, b_mod):
    depth, d, d6 = w_mod.shape
    nt = d6 // d
    return pl.pallas_call(
        _mod_body,
        grid=(depth, nt),
        in_specs=[pl.BlockSpec((SUBLANES, d), lambda l, j: (0, 0)),
                  pl.BlockSpec((1, d, d), lambda l, j: (l, 0, j)),
                  pl.BlockSpec((1, 1, d), lambda l, j: (l, 0, j))],
        out_specs=pl.BlockSpec((1, SUBLANES, d), lambda l, j: (l, 0, j)),
        out_shape=jax.ShapeDtypeStruct((depth, SUBLANES, d6), jnp.float32),
        compiler_params=_cparams(("arbitrary", "arbitrary")),
        name="adaln_mod",
    )(cvec, w_mod, b_mod.reshape(depth, 1, d6))


def _mod_row(i, blocks_per_batch, ctx_blocks):
    b = i // blocks_per_batch
    j = i - b * blocks_per_batch
    return jnp.where(j < ctx_blocks, 2, b)


def _norm_mod(x, g, shift, scale):
    y = x * lax.rsqrt(jnp.mean(x * x, axis=-1, keepdims=True) + EPS)
    return (y * g) * (1.0 + scale) + shift


def _swap_halves(t):
    lane = lax.broadcasted_iota(jnp.int32, t.shape, 1)
    first = (lane % 64) < 32
    return jnp.where(first, pltpu.roll(t, 96, axis=1), pltpu.roll(t, 32, axis=1))


def _in_body(x_ref, mod_ref, g_ref, w_ref, cos_ref, sin_ref,
             u_ref, q_ref, k_ref, v_ref, gg_ref, gs_ref, gr_ref):
    m = mod_ref[0]
    h = _norm_mod(x_ref[...], g_ref[...], m[0:1], m[1:2]).astype(jnp.bfloat16)

    def proj(lo, hi):
        return jnp.dot(h, w_ref[:, lo:hi], preferred_element_type=jnp.float32)

    o = 0
    u_ref[...] = proj(o, o + SSM_WIDTH).astype(u_ref.dtype)
    o += SSM_WIDTH
    cos = cos_ref[...]
    sin = sin_ref[...]
    k_scale = RET_HEAD_DIM ** -0.5
    for dst, scl in ((q_ref, 1.0), (k_ref, k_scale)):
        t = proj(o, o + RET_WIDTH)
        for hd in range(RET_HEADS):
            th = t[:, hd * RET_HEAD_DIM:(hd + 1) * RET_HEAD_DIM]
            r = th * cos + _swap_halves(th) * sin
            if scl != 1.0:
                r = r * scl
            dst[:, hd * RET_HEAD_DIM:(hd + 1) * RET_HEAD_DIM] = r.astype(dst.dtype)
        o += RET_WIDTH
    v_ref[...] = proj(o, o + RET_WIDTH).astype(v_ref.dtype)
    o += RET_WIDTH
    gg_ref[...] = proj(o, o + RET_WIDTH).astype(gg_ref.dtype)
    o += RET_WIDTH
    gs_ref[...] = proj(o, o + D_MODEL).astype(gs_ref.dtype)
    o += D_MODEL
    gr_ref[...] = proj(o, o + D_MODEL).astype(gr_ref.dtype)


def _in_proj(xs, modl, g1, w_in_bf, cos_t, sin_t, blocks_per_batch, ctx_blocks):
    t, d = xs.shape
    tb = TOK_BLOCK
    row = lambda i: (i, 0)
    bf = jnp.bfloat16
    outs = [jax.ShapeDtypeStruct((t, SSM_WIDTH), bf)] + [jax.ShapeDtypeStruct((t, RET_WIDTH), bf)] * 4 \
        + [jax.ShapeDtypeStruct((t, D_MODEL), bf)] * 2
    return pl.pallas_call(
        _in_body,
        grid=(t // tb,),
        in_specs=[pl.BlockSpec((tb, d), row),
                  pl.BlockSpec((1, 6, d), lambda i: (_mod_row(i, blocks_per_batch, ctx_blocks), 0, 0)),
                  pl.BlockSpec((1, d), lambda i: (0, 0)),
                  pl.BlockSpec((d, IN_COLS), lambda i: (0, 0)),
                  pl.BlockSpec((tb, RET_HEAD_DIM), row),
                  pl.BlockSpec((tb, RET_HEAD_DIM), row)],
        out_specs=[pl.BlockSpec((tb, SSM_WIDTH), row)] + [pl.BlockSpec((tb, RET_WIDTH), row)] * 4
        + [pl.BlockSpec((tb, D_MODEL), row)] * 2,
        out_shape=outs,
        compiler_params=_cparams(("arbitrary",)),
        name="in_proj",
    )(xs, modl, g1, w_in_bf, cos_t, sin_t)


def _s5_operators(B_re, B_im, C_re, C_im, d_skip, lam_re_f, lam_im_f, log_dt_f,
                  lam_re_b, lam_im_b, log_dt_b):
    f32 = jnp.float32
    hp = lax.Precision.HIGHEST
    T = S5_CHUNK
    ks = jnp.arange(T + 1, dtype=f32)[:, None, None]

    def direction(lam_re, lam_im, log_dt):
        dt = jnp.exp(log_dt.astype(f32))[:, None]
        ar, ai = lam_re.astype(f32) * dt, lam_im.astype(f32) * dt
        mag = jnp.exp(ks * ar)
        pw_re, pw_im = mag * jnp.cos(ks * ai), mag * jnp.sin(ks * ai)
        x, y = pw_re[1] - 1.0, pw_im[1]
        den = lam_re * lam_re + lam_im * lam_im
        bf_re, bf_im = (x * lam_re + y * lam_im) / den, (y * lam_re - x * lam_im) / den
        bt_re = bf_re[..., None] * B_re - bf_im[..., None] * B_im
        bt_im = bf_re[..., None] * B_im + bf_im[..., None] * B_re
        cp_re = C_re[None] * pw_re[:, :, None, :] - C_im[None] * pw_im[:, :, None, :]
        cp_im = C_re[None] * pw_im[:, :, None, :] + C_im[None] * pw_re[:, :, None, :]
        taps = (jnp.einsum('kgjp,gpi->kgij', cp_re[:T], bt_re, precision=hp)
                - jnp.einsum('kgjp,gpi->kgij', cp_im[:T], bt_im, precision=hp))
        zw_re = pw_re[:T, :, :, None] * bt_re[None] - pw_im[:T, :, :, None] * bt_im[None]
        zw_im = pw_re[:T, :, :, None] * bt_im[None] + pw_im[:T, :, :, None] * bt_re[None]
        return taps, (zw_re, zw_im), (cp_re, cp_im), (pw_re[T], pw_im[T])

    taps_f, zw_f, cp_f, a_f = direction(lam_re_f, lam_im_f, log_dt_f)
    taps_b, zw_b, cp_b, a_b = direction(lam_re_b, lam_im_b, log_dt_b)
    G, I = SSM_GROUPS, SSM_GROUP
    dd = d_skip.astype(f32).reshape(G, I)
    center = taps_f[0] + taps_b[0] + dd[:, :, None] * jnp.eye(I, dtype=f32)[None]
    full = jnp.concatenate([taps_b[:0:-1], center[None], taps_f[1:]], axis=0)
    full = full.astype(jnp.bfloat16)
    s_i = jnp.arange(T)
    shift = ((s_i[None, :] - s_i[:, None]) + T - 1)[None] == jnp.arange(2 * T - 1)[:, None, None]
    m_op = jnp.einsum('dst,dgij->gsitj', shift.astype(jnp.bfloat16), full,
                      preferred_element_type=jnp.bfloat16).reshape(G, T * I, T * I)

    def zcols(zw, flip):
        re, im = zw
        if flip:
            re, im = re[::-1], im[::-1]
        f = lambda a: a.transpose(1, 0, 3, 2).reshape(G, T * I, SSM_STATE)
        return [f(re), f(im)]

    v_op = jnp.concatenate(zcols(zw_f, True) + zcols(zw_b, False), axis=-1)

    def wrows(cp, idx):
        re, im = cp
        f = lambda a: a[idx].transpose(1, 3, 0, 2).reshape(G, SSM_STATE, T * I)
        return [f(re), -f(im)]

    w_op = jnp.concatenate(wrows(cp_f, jnp.arange(1, T + 1)) + wrows(cp_b, T - jnp.arange(T)), axis=1)
    a1 = jnp.stack([a_f[0], a_f[0], a_b[0], a_b[0]]).reshape(4, G * SSM_STATE)
    a2 = jnp.stack([-a_f[1], a_f[1], -a_b[1], a_b[1]]).reshape(4, G * SSM_STATE)
    return (m_op, v_op.astype(jnp.bfloat16)), w_op.astype(jnp.bfloat16), a1, a2


def _s5_intra_body(u_ref, m_ref, v_ref, y_ref, z_ref):
    u = u_ref[0]
    y_ref[0] = jnp.dot(u, m_ref[0], preferred_element_type=jnp.float32)
    z_ref[0] = jnp.dot(u, v_ref[0], preferred_element_type=jnp.float32)


def _s5_intra(ug, mv):
    m_op, v_op = mv
    g, m, kdim = ug.shape
    n, nz = m_op.shape[-1], v_op.shape[-1]
    return pl.pallas_call(
        _s5_intra_body,
        grid=(g,),
        in_specs=[pl.BlockSpec((1, m, kdim), lambda i: (i, 0, 0)),
                  pl.BlockSpec((1, kdim, n), lambda i: (i, 0, 0)),
                  pl.BlockSpec((1, kdim, nz), lambda i: (i, 0, 0))],
        out_specs=[pl.BlockSpec((1, m, n), lambda i: (i, 0, 0)),
                   pl.BlockSpec((1, m, nz), lambda i: (i, 0, 0))],
        out_shape=[jax.ShapeDtypeStruct((g, m, n), jnp.float32),
                   jax.ShapeDtypeStruct((g, m, nz), jnp.float32)],
        compiler_params=_cparams(("arbitrary",)),
        name="s5_intra",
    )(ug, m_op, v_op)


def _s5_scan_body(z_ref, a1_ref, a2_ref, s_ref):
    steps = z_ref.shape[0]
    a1 = a1_ref[...]
    a2 = a2_ref[...]
    row = lax.broadcasted_iota(jnp.int32, a1.shape, 0)
    even = (row % 2) == 0

    def step(k, s):
        s_ref[k] = s
        partner = jnp.where(even, pltpu.roll(s, SUBLANES - 1, axis=0), pltpu.roll(s, 1, axis=0))
        return a1 * s + a2 * partner + z_ref[k]

    lax.fori_loop(0, steps, step, jnp.zeros(a1.shape, jnp.float32))


def _s5_scan(z8, a1, a2):
    steps, r, n = z8.shape
    cb = 512
    return pl.pallas_call(
        _s5_scan_body,
        grid=(n // cb,),
        in_specs=[pl.BlockSpec((steps, r, cb), lambda i: (0, 0, i)),
                  pl.BlockSpec((r, cb), lambda i: (0, i)),
                  pl.BlockSpec((r, cb), lambda i: (0, i))],
        out_specs=pl.BlockSpec((steps, r, cb), lambda i: (0, 0, i)),
        out_shape=jax.ShapeDtypeStruct((steps, r, n), jnp.float32),
        compiler_params=_cparams(("arbitrary",)),
        name="s5_scan",
    )(z8, a1, a2)


def _s5_out_body(y_ref, s_ref, w_ref, o_ref):
    o_ref[0] = (y_ref[0] + jnp.dot(s_ref[0], w_ref[0], preferred_element_type=jnp.float32)).astype(o_ref.dtype)


def _s5_out(yi, sg, w_op):
    g, m, n = yi.shape
    ks = sg.shape[-1]
    return pl.pallas_call(
        _s5_out_body,
        grid=(g,),
        in_specs=[pl.BlockSpec((1, m, n), lambda i: (i, 0, 0)),
                  pl.BlockSpec((1, m, ks), lambda i: (i, 0, 0)),
                  pl.BlockSpec((1, ks, n), lambda i: (i, 0, 0))],
        out_specs=pl.BlockSpec((1, m, n), lambda i: (i, 0, 0)),
        out_shape=jax.ShapeDtypeStruct((g, m, n), jnp.bfloat16),
        compiler_params=_cparams(("arbitrary",)),
        name="s5_out",
    )(yi, sg, w_op)


def _s5_mix(u, ops, nb, lb, lc):
    mv, w_op, a1, a2 = ops
    G, I, P, T = SSM_GROUPS, SSM_GROUP, SSM_STATE, S5_CHUNK
    nch = lb // T
    cch = lc // T
    ug = u.reshape(nb * nch, T, G, I).transpose(2, 0, 1, 3).reshape(G, nb * nch, T * I)
    yi, z = _s5_intra(ug, mv)
    order_b = jnp.concatenate([jnp.arange(cch - 1, -1, -1), jnp.arange(nch - 1, cch - 1, -1)])
    z6 = z.reshape(G, nb, nch, 2, 2, P)
    zf = z6[:, :, :, 0]
    zb = z6[:, :, order_b, 1]
    z8 = jnp.stack([zf, zb], axis=3)
    z8 = z8.transpose(2, 1, 3, 4, 0, 5).reshape(nch, nb * 4, G * P)
    rows = nb * 4
    pad = (-rows) % SUBLANES
    reps = (rows + pad) // 4
    if pad:
        z8 = jnp.pad(z8, ((0, 0), (0, pad), (0, 0)))
    s8 = _s5_scan(z8, jnp.tile(a1, (reps, 1)), jnp.tile(a2, (reps, 1)))[:, :rows]
    s6 = s8.reshape(nch, nb, 2, 2, G, P).transpose(4, 1, 0, 2, 3, 5)
    inv_b = jnp.argsort(order_b)
    sf = s6[:, :, :, 0]
    sb = s6[:, :, inv_b, 1]
    sg = jnp.stack([sf, sb], axis=3).reshape(G, nb * nch, 4 * P).astype(jnp.bfloat16)
    y = _s5_out(yi, sg, w_op)
    return y.reshape(G, nb * nch, T, I).transpose(1, 2, 0, 3).reshape(nb * lb, G * I)


def _ret_consts(ret_decay_f, ret_decay_b):
    f32 = jnp.float32
    C = RET_CHUNK
    lg_f = -jnp.exp(ret_decay_f.astype(f32))[:, None, None]
    lg_b = -jnp.exp(ret_decay_b.astype(f32))[:, None, None]
    pos = jnp.arange(C, dtype=f32)
    diff = pos[:, None] - pos[None, :]
    dmat = jnp.where(diff >= 0, jnp.exp(jnp.where(diff >= 0, diff, 0.0)[None] * lg_f),
                     jnp.exp(jnp.where(diff < 0, -diff, 0.0)[None] * lg_b))
    col = lambda e: jnp.broadcast_to(jnp.exp(e), (RET_HEADS, C, RET_HEAD_DIM))
    p1 = pos[None, :, None]
    qdec_f = col((p1 + 1.0) * lg_f)
    kdec_f = col((C - 1.0 - p1) * lg_f)
    qdec_b = col((C - p1) * lg_b)
    kdec_b = col(p1 * lg_b)
    cd_f = col(jnp.full_like(p1, C) * lg_f)
    cd_b = col(jnp.full_like(p1, C) * lg_b)
    return dmat, qdec_f, kdec_f, cd_f, qdec_b, kdec_b, cd_b


def _ret_state_update(s_ref, b, hd, kh, vh, kdec, cd):
    kd = (kh.astype(jnp.float32) * kdec).astype(jnp.bfloat16)
    inc = lax.dot_general(kd, vh, (((0,), (0,)), ((), ())), preferred_element_type=jnp.float32)
    s_ref[b, hd] = cd * s_ref[b, hd] + inc


def _ret_fwd_body(q_ref, k_ref, v_ref, dm_ref, qd_ref, kd_ref, cd_ref, o_ref, s_ref):
    @pl.when(pl.program_id(0) == 0)
    def _():
        s_ref[...] = jnp.zeros(s_ref.shape, s_ref.dtype)

    for b in range(q_ref.shape[0]):
        for hd in range(RET_HEADS):
            sl = slice(hd * RET_HEAD_DIM, (hd + 1) * RET_HEAD_DIM)
            qh, kh, vh = q_ref[b, :, sl], k_ref[b, :, sl], v_ref[b, :, sl]
            sc = lax.dot_general(qh, kh, (((1,), (1,)), ((), ())), preferred_element_type=jnp.float32)
            p = (sc * dm_ref[hd]).astype(jnp.bfloat16)
            o = jnp.dot(p, vh, preferred_element_type=jnp.float32)
            cross = jnp.dot(qh, s_ref[b, hd].astype(jnp.bfloat16), preferred_element_type=jnp.float32)
            o_ref[b, :, sl] = o + cross * qd_ref[hd]
            _ret_state_update(s_ref, b, hd, kh, vh, kd_ref[hd], cd_ref[hd])


def _ret_bwd_body(q_ref, k_ref, v_ref, of_ref, g_ref, qd_ref, kd_ref, cd_ref, y_ref, s_ref):
    @pl.when(pl.program_id(0) == 0)
    def _():
        s_ref[...] = jnp.zeros(s_ref.shape, s_ref.dtype)

    for b in range(q_ref.shape[0]):
        for hd in range(RET_HEADS):
            sl = slice(hd * RET_HEAD_DIM, (hd + 1) * RET_HEAD_DIM)
            qh, kh, vh = q_ref[b, :, sl], k_ref[b, :, sl], v_ref[b, :, sl]
            cross = jnp.dot(qh, s_ref[b, hd].astype(jnp.bfloat16), preferred_element_type=jnp.float32)
            o = of_ref[b, :, sl] + cross * qd_ref[hd]
            o = o * lax.rsqrt(jnp.mean(o * o, axis=-1, keepdims=True) + EPS)
            gv = g_ref[b, :, sl].astype(jnp.float32)
            y_ref[b, :, sl] = (o * (gv * jax.nn.sigmoid(gv))).astype(y_ref.dtype)
            _ret_state_update(s_ref, b, hd, kh, vh, kd_ref[hd], cd_ref[hd])


def _retention(q, k, v, g, consts, nb, lb, lc):
    dmat, qdec_f, kdec_f, cd_f, qdec_b, kdec_b, cd_b = consts
    C, W = RET_CHUNK, RET_WIDTH
    nblk, cblk = lb // C, lc // C
    r3 = lambda a: a.reshape(nb, lb, W)
    fwd_idx = lambda s: (0, s, 0)
    bwd_idx = lambda s: (0, jnp.where(s < cblk, cblk - 1 - s, nblk - 1 + cblk - s), 0)
    cst = pl.BlockSpec((RET_HEADS, C, RET_HEAD_DIM), lambda s: (0, 0, 0))
    state = pltpu.VMEM((nb, RET_HEADS, RET_HEAD_DIM, RET_HEAD_DIM), jnp.float32)
    o_f = pl.pallas_call(
        _ret_fwd_body,
        grid=(nblk,),
        in_specs=[pl.BlockSpec((nb, C, W), fwd_idx)] * 3 + [cst] * 4,
        out_specs=pl.BlockSpec((nb, C, W), fwd_idx),
        out_shape=jax.ShapeDtypeStruct((nb, lb, W), jnp.float32),
        scratch_shapes=[state],
        compiler_params=_cparams(("arbitrary",)),
        name="ret_fwd",
    )(r3(q), r3(k), r3(v), dmat, qdec_f, kdec_f, cd_f)
    y = pl.pallas_call(
        _ret_bwd_body,
        grid=(nblk,),
        in_specs=[pl.BlockSpec((nb, C, W), bwd_idx)] * 5 + [cst] * 3,
        out_specs=pl.BlockSpec((nb, C, W), bwd_idx),
        out_shape=jax.ShapeDtypeStruct((nb, lb, W), jnp.bfloat16),
        scratch_shapes=[state],
        compiler_params=_cparams(("arbitrary",)),
        name="ret_bwd",
    )(r3(q), r3(k), r3(v), o_f, r3(g), qdec_b, kdec_b, cd_b)
    return y.reshape(nb * lb, W)


def _gelu_tanh(x):
    return 0.5 * x * (1.0 + jnp.tanh(math.sqrt(2.0 / math.pi) * (x + 0.044715 * x * x * x)))


def _merge_body(x_ref, ys_ref, yr_ref, gs_ref, gr_ref, mod_ref, wg_ref, wr_ref, wo_ref, o_ref):
    f32 = jnp.float32
    a_in = _gelu_tanh(ys_ref[...].astype(f32)).astype(jnp.bfloat16)
    ab = jnp.dot(a_in, wg_ref[...], preferred_element_type=f32)
    ys = ab[:, :D_MODEL] * jax.nn.sigmoid(ab[:, D_MODEL:])
    yr = jnp.dot(yr_ref[...], wr_ref[...], preferred_element_type=f32)
    m = jax.nn.sigmoid(gs_ref[...].astype(f32)) * ys + jax.nn.sigmoid(gr_ref[...].astype(f32)) * yr
    y = jnp.dot(m.astype(jnp.bfloat16), wo_ref[...], preferred_element_type=f32)
    o_ref[...] = x_ref[...] + mod_ref[0][2:3] * y


def _merge(xs, ys, yr, gs, gr, modl, wg, wr, wo, blocks_per_batch, ctx_blocks):
    t, d = xs.shape
    tb = TOK_BLOCK
    row = lambda i: (i, 0)
    full = lambda a: pl.BlockSpec(a.shape, lambda i: (0, 0))
    return pl.pallas_call(
        _merge_body,
        grid=(t // tb,),
        in_specs=[pl.BlockSpec((tb, d), row), pl.BlockSpec((tb, SSM_WIDTH), row),
                  pl.BlockSpec((tb, RET_WIDTH), row), pl.BlockSpec((tb, d), row), pl.BlockSpec((tb, d), row),
                  pl.BlockSpec((1, 6, d), lambda i: (_mod_row(i, blocks_per_batch, ctx_blocks), 0, 0)),
                  full(wg), full(wr), full(wo)],
        out_specs=pl.BlockSpec((tb, d), row),
        out_shape=jax.ShapeDtypeStruct((t, d), jnp.float32),
        compiler_params=_cparams(("arbitrary",)),
        name="merge_out",
    )(xs, ys, yr, gs, gr, modl, wg, wr, wo)


def _top16_rows(vals, payload=None):
    n = vals.shape[0]
    rid = lax.broadcasted_iota(jnp.int32, vals.shape, 0).astype(jnp.float32)
    top_v, top_p = [], []
    for _ in range(PEER_TOPK):
        m = jnp.max(vals, axis=0, keepdims=True)
        first = jnp.min(jnp.where(vals == m, rid, float(n)), axis=0, keepdims=True)
        hit = rid == first
        top_v.append(m)
        top_p.append(first if payload is None else
                     jnp.max(jnp.where(hit, payload, -1.0), axis=0, keepdims=True))
        vals = jnp.where(hit, -jnp.inf, vals)
    return jnp.concatenate(top_v, axis=0), jnp.concatenate(top_p, axis=0)


def _pair_candidates(s1, i1, s2, i2):
    k = PEER_TOPK
    bid = lax.broadcasted_iota(jnp.int32, (SUBLANES, s1.shape[1]), 0)
    cs, ce = [s1[0:1] + s2], [i1[0:1] * float(PEER_NKEYS) + i2]
    for a in range(1, SUBLANES):
        keep = bid < (k // (a + 1))
        cs.append(jnp.where(keep, s1[a:a + 1] + s2[0:SUBLANES], -jnp.inf))
        ce.append(i1[a:a + 1] * float(PEER_NKEYS) + i2[0:SUBLANES])
    cs.append(s1[SUBLANES:k] + s2[0:1])
    ce.append(i1[SUBLANES:k] * float(PEER_NKEYS) + i2[0:1])
    return jnp.concatenate(cs, axis=0), jnp.concatenate(ce, axis=0)


def _route_body(x_ref, mod_ref, g_ref, wq_ref, keys_ref, h_ref, idx_ref, gate_ref):
    m = mod_ref[0]
    h2 = _norm_mod(x_ref[...], g_ref[...], m[3:4], m[4:5])
    h_ref[...] = h2
    q = jnp.dot(h2.astype(jnp.bfloat16), wq_ref[...], preferred_element_type=jnp.float32)
    idx_rows, gate_rows = [], []
    for hd in range(PEER_HEADS):
        qh = q[:, hd * PEER_DKEY:(hd + 1) * PEER_DKEY].astype(jnp.bfloat16)
        tops = []
        for s in range(2):
            st = lax.dot_general(keys_ref[hd, s], qh, (((1,), (1,)), ((), ())),
                                 preferred_element_type=jnp.float32)
            tops.append(_top16_rows(st))
        (s1, i1), (s2, i2) = tops
        best_s, best_e = _top16_rows(*_pair_candidates(s1, i1, s2, i2))
        ex = jnp.exp(best_s - best_s[0:1])
        gate_rows.append(ex / jnp.sum(ex, axis=0, keepdims=True))
        idx_rows.append(best_e)
    idx_ref[...] = jnp.concatenate(idx_rows, axis=0).T.astype(jnp.int32)
    gate_ref[...] = jnp.concatenate(gate_rows, axis=0).T


def _route(xs, modl, g2, wq_bf, keys_pad, blocks_per_batch, ctx_blocks):
    t, d = xs.shape
    tb = PEER_ROUTE_BLOCK
    row = lambda i: (i, 0)
    return pl.pallas_call(
        _route_body,
        grid=(t // tb,),
        in_specs=[pl.BlockSpec((tb, d), row),
                  pl.BlockSpec((1, 6, d), lambda i: (_mod_row(i, blocks_per_batch, ctx_blocks), 0, 0)),
                  pl.BlockSpec((1, d), lambda i: (0, 0)),
                  pl.BlockSpec((d, PEER_HEADS * PEER_DKEY), lambda i: (0, 0)),
                  pl.BlockSpec(keys_pad.shape, lambda i: (0, 0, 0, 0))],
        out_specs=[pl.BlockSpec((tb, d), row), pl.BlockSpec((tb, PICKS), row), pl.BlockSpec((tb, PICKS), row)],
        out_shape=[jax.ShapeDtypeStruct((t, d), jnp.float32),
                   jax.ShapeDtypeStruct((t, PICKS), jnp.int32), jax.ShapeDtypeStruct((t, PICKS), jnp.float32)],
        compiler_params=_cparams(("arbitrary",)),
        name="peer_route",
    )(xs, modl, g2, wq_bf, keys_pad)


def _pack_table(tab):
    n, d = tab.shape
    halves = tab.reshape(n, 2, d // 2 // LANES, LANES)
    bits = lax.bitcast_convert_type(halves.astype(jnp.bfloat16), jnp.uint16).astype(jnp.uint32)
    return lax.bitcast_convert_type((bits[:, 0] << 16) | bits[:, 1], jnp.int32)


_NT = (((1,), (1,)), ((), ()))
GATHER_PARTS = 4
IDX_OFFSETS = 8


def _hi_lo_rows(first, second, rid):
    f_hi = first.astype(jnp.bfloat16).astype(jnp.float32)
    s_hi = second.astype(jnp.bfloat16).astype(jnp.float32)
    v = jnp.where(rid == 0, f_hi, jnp.where(rid == 1, first - f_hi,
                  jnp.where(rid == 2, s_hi, jnp.where(rid == 3, second - s_hi, 0.0))))
    return v.astype(jnp.bfloat16)


def _gather_part(idx_ref, offs, tab_ref, st_ref, slot, t, part):
    n = PICKS // GATHER_PARTS
    nrow = tab_ref.shape[1]
    k = len(offs)
    for p0 in range(part * n, (part + 1) * n, k):
        sub = idx_ref.at[t, pl.ds(p0, k)]
        for j in range(k):
            st_ref[slot, pl.ds(p0 + j, nrow, stride=SLAB_STRIDE), :] = tab_ref[sub[offs[j]]]


def _slab(st_ref, slot, r):
    return pltpu.bitcast(st_ref[slot, r * SLAB_STRIDE:r * SLAB_STRIDE + PICKS, :], jnp.bfloat16)


def _token_pipeline(n_tok, gather_part, compute_part, finish):
    g = PEER_GROUP

    def step(slot, t, prev_slot, t_prev):
        acc = None
        for part in range(GATHER_PARTS):
            gather_part(slot, t, part)
            if prev_slot is not None:
                acc = compute_part(prev_slot, t_prev, part, acc)
        if prev_slot is not None:
            finish(t_prev, acc)

    step(0, 0, None, None)
    for j in range(1, g):
        step(j, j, j - 1, j - 1)

    def group(gi, carry):
        t0 = gi * g
        step(0, t0, g - 1, t0 - 1)
        for j in range(1, g):
            step(j, t0 + j, j - 1, t0 + j - 1)
        return carry

    lax.fori_loop(1, n_tok // g, group, 0)
    acc = None
    for part in range(GATHER_PARTS):
        acc = compute_part(g - 1, n_tok - 1, part, acc)
    finish(n_tok - 1, acc)


def _expert_act_body(idx_ref, off_ref, h_ref, gate_ref, tab_ref, w_ref, st_ref):
    rid = lax.broadcasted_iota(jnp.int32, (SUBLANES, LANES), 0)
    lane = lax.broadcasted_iota(jnp.int32, (1, 2 * PICKS), 1)
    even = (lane % 2) == 0

    def compute_part(slot, t, r, acc):
        half = h_ref.shape[1] // 2
        hrow = h_ref[pl.ds(t, 1), :]
        xh = jnp.broadcast_to(hrow[:, r * LANES:(r + 1) * LANES], rid.shape)
        xl = jnp.broadcast_to(hrow[:, half + r * LANES:half + (r + 1) * LANES], rid.shape)
        res = lax.dot_general(_hi_lo_rows(xh, xl, rid), _slab(st_ref, slot, r), _NT,
                              preferred_element_type=jnp.float32)
        return res if acc is None else acc + res

    def finish(t, acc):
        v = jnp.where(even, acc[2:3] + acc[3:4], acc[0:1] + acc[1:2])
        act = v + jnp.where(even, pltpu.roll(v, 2 * PICKS - 1, axis=1), pltpu.roll(v, 1, axis=1))
        gl = 0.5 * act * (1.0 + lax.erf(act * (1.0 / math.sqrt(2.0))))
        w_ref[pl.ds(t, 1), :] = gate_ref[pl.ds(t, 1), :] * gl

    offs = [off_ref[j] for j in range(IDX_OFFSETS)]
    _token_pipeline(h_ref.shape[0], partial(_gather_part, idx_ref, offs, tab_ref, st_ref), compute_part, finish)


def _expert_out_body(idx_ref, off_ref, w_ref, tab_ref, x_ref, mod_ref, o_ref, st_ref):
    rid = lax.broadcasted_iota(jnp.int32, (SUBLANES, 2 * PICKS), 0)
    lane = lax.broadcasted_iota(jnp.int32, (SUBLANES, 2 * PICKS), 1)
    even = (lane % 2) == 0

    def compute_part(slot, t, r, acc):
        wrow = jnp.broadcast_to(w_ref[pl.ds(t, 1), :], rid.shape)
        lhs = _hi_lo_rows(jnp.where(even, 0.0, wrow), jnp.where(even, wrow, 0.0), rid)
        res = jnp.dot(lhs, _slab(st_ref, slot, r), preferred_element_type=jnp.float32)
        first, second = acc if acc is not None else ([], [])
        return first + [res[0:1] + res[1:2]], second + [res[2:3] + res[3:4]]

    g2 = mod_ref[0][5:6]

    def finish(t, acc):
        o_ref[pl.ds(t, 1), :] = x_ref[pl.ds(t, 1), :] + g2 * jnp.concatenate(acc[0] + acc[1], axis=1)

    offs = [off_ref[j] for j in range(IDX_OFFSETS)]
    _token_pipeline(o_ref.shape[0], partial(_gather_part, idx_ref, offs, tab_ref, st_ref), compute_part, finish)


def _expert_specs(t):
    nt = PEER_TOK_BLOCK
    nrow = D_MODEL // 2 // LANES
    assert nrow == GATHER_PARTS and nt % PEER_GROUP == 0 and nt // PEER_GROUP >= 2
    smem = pl.BlockSpec((nt, PICKS), lambda i: (i, 0), memory_space=pltpu.SMEM)
    vrow = pl.BlockSpec((nt, 2 * PICKS), lambda i: (i, 0))
    full = pl.BlockSpec((nt, D_MODEL), lambda i: (i, 0))
    table = pl.BlockSpec(memory_space=pltpu.VMEM)
    stage = pltpu.VMEM((PEER_GROUP, nrow * SLAB_STRIDE, LANES), jnp.int32)
    assert PICKS // GATHER_PARTS % IDX_OFFSETS == 0
    offs = pl.BlockSpec(memory_space=pltpu.SMEM)
    return nt, smem, offs, vrow, full, table, stage


def _expert_act(idx, h2, gate2, tab_u):
    t = idx.shape[0]
    nt, smem, offs, vrow, full, table, stage = _expert_specs(t)
    return pl.pallas_call(
        _expert_act_body,
        grid=(t // nt,),
        in_specs=[smem, offs, full, vrow, table],
        out_specs=vrow,
        out_shape=jax.ShapeDtypeStruct((t, 2 * PICKS), jnp.float32),
        scratch_shapes=[stage],
        compiler_params=_cparams(("arbitrary",)),
        name="peer_act",
    )(idx, jnp.arange(IDX_OFFSETS, dtype=jnp.int32), h2, gate2, tab_u)


def _expert_out(idx, w2, tab_v, xs, modl, blocks_per_batch, ctx_blocks):
    t = idx.shape[0]
    nt, smem, offs, vrow, full, table, stage = _expert_specs(t)
    mod = pl.BlockSpec((1, 6, D_MODEL), lambda i: (_mod_row(i, blocks_per_batch, ctx_blocks), 0, 0))
    return pl.pallas_call(
        _expert_out_body,
        grid=(t // nt,),
        in_specs=[smem, offs, vrow, table, full, mod],
        out_specs=full,
        out_shape=jax.ShapeDtypeStruct((t, D_MODEL), jnp.float32),
        scratch_shapes=[stage],
        compiler_params=_cparams(("arbitrary",)),
        name="peer_out",
    )(idx, jnp.arange(IDX_OFFSETS, dtype=jnp.int32), w2, tab_v, xs, modl)


def _final_norm_body(x_ref, g_ref, o_ref):
    xf = x_ref[...]
    o_ref[...] = xf * lax.rsqrt(jnp.mean(xf * xf, axis=-1, keepdims=True) + EPS) * g_ref[...]


def _final_norm(x2, g):
    rows, d = x2.shape
    tm = TOK_BLOCK
    return pl.pallas_call(
        _final_norm_body,
        grid=(rows // tm,),
        in_specs=[pl.BlockSpec((tm, d), lambda i: (i, 0)), pl.BlockSpec((1, d), lambda i: (0, 0))],
        out_specs=pl.BlockSpec((tm, d), lambda i: (i, 0)),
        out_shape=jax.ShapeDtypeStruct((rows, d), x2.dtype),
        compiler_params=_cparams(("arbitrary",)),
        name="final_norm",
    )(x2, g.reshape(1, d))


def _rope_tables(seq, lc, nb):
    rows = seq // GRID_W
    row_ids = jnp.repeat(jnp.arange(rows, dtype=jnp.float32), GRID_W)
    col_ids = jnp.tile(jnp.arange(GRID_W, dtype=jnp.float32), rows)
    n_freq = RET_HEAD_DIM // 4
    freqs = ROPE_BASE ** (-jnp.arange(n_freq, dtype=jnp.float32) / n_freq)
    ang_r, ang_c = row_ids[:, None] * freqs[None], col_ids[:, None] * freqs[None]
    cos = jnp.concatenate([jnp.cos(ang_r), jnp.cos(ang_r), jnp.cos(ang_c), jnp.cos(ang_c)], axis=-1)
    sin = jnp.concatenate([-jnp.sin(ang_r), jnp.sin(ang_r), -jnp.sin(ang_c), jnp.sin(ang_c)], axis=-1)
    cos = jnp.concatenate([jnp.ones((lc, RET_HEAD_DIM), jnp.float32), cos], axis=0)
    sin = jnp.concatenate([jnp.zeros((lc, RET_HEAD_DIM), jnp.float32), sin], axis=0)
    return jnp.tile(cos, (nb, 1)), jnp.tile(sin, (nb, 1))


def kernel(x, c, ctx, c_ctx, w_mod, b_mod, norm1_g, norm2_g, w_in, ssm_B_re, ssm_B_im,
           ssm_C_re, ssm_C_im, ssm_D, ssm_lam_re_f, ssm_lam_im_f, ssm_log_dt_f,
           ssm_lam_re_b, ssm_lam_im_b, ssm_log_dt_b, w_ssm_glu, ret_decay_f, ret_decay_b,
           w_ret_up, w_out, peer_w_q, peer_sub_keys, peer_u, peer_v, final_norm_g):
    nb, seq, d = x.shape
    lc = ctx.shape[1]
    depth = w_mod.shape[0]
    lb = lc + seq
    bf = jnp.bfloat16
    assert nb == 2 and d == D_MODEL and lc % TOK_BLOCK == 0 and seq % TOK_BLOCK == 0
    bpb, cbl = lb // TOK_BLOCK, lc // TOK_BLOCK
    bpb_r, cbl_r = lb // PEER_ROUTE_BLOCK, lc // PEER_ROUTE_BLOCK

    xs = jnp.concatenate([ctx, x], axis=1).reshape(nb * lb, d)
    cvec = jnp.zeros((SUBLANES, d), jnp.float32).at[:nb].set(c).at[nb].set(c_ctx)
    mod_all = _modulation(cvec, w_mod, b_mod)[:, :nb + 1].reshape(depth, nb + 1, 6, d)
    cos_t, sin_t = _rope_tables(seq, lc, nb)
    half = PEER_DKEY // 2

    for i in range(depth):
        modl = mod_all[i]
        s5_ops = _s5_operators(ssm_B_re[i], ssm_B_im[i], ssm_C_re[i], ssm_C_im[i], ssm_D[i],
                               ssm_lam_re_f[i], ssm_lam_im_f[i], ssm_log_dt_f[i],
                               ssm_lam_re_b[i], ssm_lam_im_b[i], ssm_log_dt_b[i])
        ret_c = _ret_consts(ret_decay_f[i], ret_decay_b[i])
        u, q, k, v, g, gs, gr = _in_proj(xs, modl, norm1_g[i].reshape(1, d), w_in[i].astype(bf),
                                         cos_t, sin_t, bpb, cbl)
        ys = _s5_mix(u, s5_ops, nb, lb, lc)
        yr = _retention(q, k, v, g, ret_c, nb, lb, lc)
        xs = _merge(xs, ys, yr, gs, gr, modl, w_ssm_glu[i].astype(bf), w_ret_up[i].astype(bf),
                    w_out[i].astype(bf), bpb, cbl)
        sk = peer_sub_keys[i].astype(bf)
        zeros = jnp.zeros_like(sk)
        keys_pad = jnp.stack([jnp.concatenate([sk[:, 0], zeros[:, 0]], axis=-1),
                              jnp.concatenate([zeros[:, 1], sk[:, 1]], axis=-1)], axis=1)
        h2, idx, gate = _route(xs, modl, norm2_g[i].reshape(1, d), peer_w_q[i].astype(bf),
                                   keys_pad, bpb_r, cbl_r)
        wts = _expert_act(idx, h2, jnp.repeat(gate, 2, axis=1), _pack_table(peer_u[i]))
        xs = _expert_out(idx, wts, _pack_table(peer_v[i]), xs, modl, lb // PEER_TOK_BLOCK, lc // PEER_TOK_BLOCK)

    lat = xs.reshape(nb, lb, d)[:, lc:].reshape(nb * seq, d)
    return _final_norm(lat, final_norm_g).reshape(nb, seq, d)
```

```python
import math
from functools import partial

import jax
import jax.numpy as jnp
from jax import lax
from jax.experimental import pallas as pl
from jax.experimental.pallas import tpu as pltpu

D_MODEL = 1024
GRID_W = 64
EPS = 1e-6
SSM_WIDTH = 512
SSM_GROUP = 16
SSM_GROUPS = SSM_WIDTH // SSM_GROUP
SSM_STATE = 64
RET_WIDTH = 512
RET_HEADS = 4
RET_HEAD_DIM = RET_WIDTH // RET_HEADS
ROPE_BASE = 10000.0
PEER_HEADS = 8
PEER_NKEYS = 128
PEER_DKEY = 128
PEER_TOPK = 16
IN_COLS = SSM_WIDTH + 4 * RET_WIDTH + 2 * D_MODEL

LANES = 128
SUBLANES = 8
VMEM_LIMIT = 56 * 1024 * 1024
TOK_BLOCK = 256
RET_CHUNK = 128
S5_CHUNK = 64
PEER_ROUTE_BLOCK = 128
PEER_TOK_BLOCK = 256
PEER_GROUP = 32
PICKS = PEER_HEADS * PEER_TOPK
SLAB_STRIDE = LANES + SUBLANES


def _cparams(sem=None):
    return pltpu.CompilerParams(dimension_semantics=sem, vmem_limit_bytes=VMEM_LIMIT)


def _mod_body(c_ref, w_ref, b_ref, o_ref):
    cv = c_ref[...]
    s = cv * jax.nn.sigmoid(cv)
    o_ref[0] = jnp.dot(s, w_ref[0], preferred_element_type=jnp.float32,
                       precision=lax.Precision.HIGHEST) + b_ref[0]


def _modulation(cvec, w_mod, b_mod):
    depth, d, d6 = w_mod.shape
    nt = d6 // d
    return pl.pallas_call(
        _mod_body,
        grid=(depth, nt),
        in_specs=[pl.BlockSpec((SUBLANES, d), lambda l, j: (0, 0)),
                  pl.BlockSpec((1, d, d), lambda l, j: (l, 0, j)),
                  pl.BlockSpec((1, 1, d), lambda l, j: (l, 0, j))],
        out_specs=pl.BlockSpec((1, SUBLANES, d), lambda l, j: (l, 0, j)),
        out_shape=jax.ShapeDtypeStruct((depth, SUBLANES, d6), jnp.float32),
        compiler_params=_cparams(("arbitrary", "arbitrary")),
        name="adaln_mod",
    )(cvec, w_mod, b_mod.reshape(depth, 1, d6))


def _mod_row(i, blocks_per_batch, ctx_blocks):
    b = i // blocks_per_batch
    j = i - b * blocks_per_batch
    return jnp.where(j < ctx_blocks, 2, b)


def _norm_mod(x, g, shift, scale):
    y = x * lax.rsqrt(jnp.mean(x * x, axis=-1, keepdims=True) + EPS)
    return (y * g) * (1.0 + scale) + shift


def _swap_halves(t):
    lane = lax.broadcasted_iota(jnp.int32, t.shape, 1)
    first = (lane % 64) < 32
    return jnp.where(first, pltpu.roll(t, 96, axis=1), pltpu.roll(t, 32, axis=1))


def _in_body(x_ref, mod_ref, g_ref, w_ref, cos_ref, sin_ref,
             u_ref, q_ref, k_ref, v_ref, gg_ref, gs_ref, gr_ref):
    m = mod_ref[0]
    h = _norm_mod(x_ref[...], g_ref[...], m[0:1], m[1:2]).astype(jnp.bfloat16)

    def proj(lo, hi):
        return jnp.dot(h, w_ref[:, lo:hi], preferred_element_type=jnp.float32)

    o = 0
    u_ref[...] = proj(o, o + SSM_WIDTH).astype(u_ref.dtype)
    o += SSM_WIDTH
    cos = cos_ref[...]
    sin = sin_ref[...]
    k_scale = RET_HEAD_DIM ** -0.5
    for dst, scl in ((q_ref, 1.0), (k_ref, k_scale)):
        t = proj(o, o + RET_WIDTH)
        for hd in range(RET_HEADS):
            th = t[:, hd * RET_HEAD_DIM:(hd + 1) * RET_HEAD_DIM]
            r = th * cos + _swap_halves(th) * sin
            if scl != 1.0:
                r = r * scl
            dst[:, hd * RET_HEAD_DIM:(hd + 1) * RET_HEAD_DIM] = r.astype(dst.dtype)
        o += RET_WIDTH
    v_ref[...] = proj(o, o + RET_WIDTH).astype(v_ref.dtype)
    o += RET_WIDTH
    gg_ref[...] = proj(o, o + RET_WIDTH).astype(gg_ref.dtype)
    o += RET_WIDTH
    gs_ref[...] = proj(o, o + D_MODEL).astype(gs_ref.dtype)
    o += D_MODEL
    gr_ref[...] = proj(o, o + D_MODEL).astype(gr_ref.dtype)


def _in_proj(xs, modl, g1, w_in_bf, cos_t, sin_t, blocks_per_batch, ctx_blocks):
    t, d = xs.shape
    tb = TOK_BLOCK
    row = lambda i: (i, 0)
    bf = jnp.bfloat16
    outs = [jax.ShapeDtypeStruct((t, SSM_WIDTH), bf)] + [jax.ShapeDtypeStruct((t, RET_WIDTH), bf)] * 4 \
        + [jax.ShapeDtypeStruct((t, D_MODEL), bf)] * 2
    return pl.pallas_call(
        _in_body,
        grid=(t // tb,),
        in_specs=[pl.BlockSpec((tb, d), row),
                  pl.BlockSpec((1, 6, d), lambda i: (_mod_row(i, blocks_per_batch, ctx_blocks), 0, 0)),
                  pl.BlockSpec((1, d), lambda i: (0, 0)),
                  pl.BlockSpec((d, IN_COLS), lambda i: (0, 0)),
                  pl.BlockSpec((tb, RET_HEAD_DIM), row),
                  pl.BlockSpec((tb, RET_HEAD_DIM), row)],
        out_specs=[pl.BlockSpec((tb, SSM_WIDTH), row)] + [pl.BlockSpec((tb, RET_WIDTH), row)] * 4
        + [pl.BlockSpec((tb, D_MODEL), row)] * 2,
        out_shape=outs,
        compiler_params=_cparams(("arbitrary",)),
        name="in_proj",
    )(xs, modl, g1, w_in_bf, cos_t, sin_t)


def _s5_operators(B_re, B_im, C_re, C_im, d_skip, lam_re_f, lam_im_f, log_dt_f,
                  lam_re_b, lam_im_b, log_dt_b):
    f32 = jnp.float32
    hp = lax.Precision.HIGHEST
    T = S5_CHUNK
    ks = jnp.arange(T + 1, dtype=f32)[:, None, None]

    def direction(lam_re, lam_im, log_dt):
        dt = jnp.exp(log_dt.astype(f32))[:, None]
        ar, ai = lam_re.astype(f32) * dt, lam_im.astype(f32) * dt
        mag = jnp.exp(ks * ar)
        pw_re, pw_im = mag * jnp.cos(ks * ai), mag * jnp.sin(ks * ai)
        x, y = pw_re[1] - 1.0, pw_im[1]
        den = lam_re * lam_re + lam_im * lam_im
        bf_re, bf_im = (x * lam_re + y * lam_im) / den, (y * lam_re - x * lam_im) / den
        bt_re = bf_re[..., None] * B_re - bf_im[..., None] * B_im
        bt_im = bf_re[..., None] * B_im + bf_im[..., None] * B_re
        cp_re = C_re[None] * pw_re[:, :, None, :] - C_im[None] * pw_im[:, :, None, :]
        cp_im = C_re[None] * pw_im[:, :, None, :] + C_im[None] * pw_re[:, :, None, :]
        taps = (jnp.einsum('kgjp,gpi->kgij', cp_re[:T], bt_re, precision=hp)
                - jnp.einsum('kgjp,gpi->kgij', cp_im[:T], bt_im, precision=hp))
        zw_re = pw_re[:T, :, :, None] * bt_re[None] - pw_im[:T, :, :, None] * bt_im[None]
        zw_im = pw_re[:T, :, :, None] * bt_im[None] + pw_im[:T, :, :, None] * bt_re[None]
        return taps, (zw_re, zw_im), (cp_re, cp_im), (pw_re[T], pw_im[T])

    taps_f, zw_f, cp_f, a_f = direction(lam_re_f, lam_im_f, log_dt_f)
    taps_b, zw_b, cp_b, a_b = direction(lam_re_b, lam_im_b, log_dt_b)
    G, I = SSM_GROUPS, SSM_GROUP
    dd = d_skip.astype(f32).reshape(G, I)
    center = taps_f[0] + taps_b[0] + dd[:, :, None] * jnp.eye(I, dtype=f32)[None]
    full = jnp.concatenate([taps_b[:0:-1], center[None], taps_f[1:]], axis=0)
    full = full.astype(jnp.bfloat16)
    s_i = jnp.arange(T)
    shift = ((s_i[None, :] - s_i[:, None]) + T - 1)[None] == jnp.arange(2 * T - 1)[:, None, None]
    m_op = jnp.einsum('dst,dgij->gsitj', shift.astype(jnp.bfloat16), full,
                      preferred_element_type=jnp.bfloat16).reshape(G, T * I, T * I)

    def zcols(zw, flip):
        re, im = zw
        if flip:
            re, im = re[::-1], im[::-1]
        f = lambda a: a.transpose(1, 0, 3, 2).reshape(G, T * I, SSM_STATE)
        return [f(re), f(im)]

    v_op = jnp.concatenate(zcols(zw_f, True) + zcols(zw_b, False), axis=-1)

    def wrows(cp, idx):
        re, im = cp
        f = lambda a: a[idx].transpose(1, 3, 0, 2).reshape(G, SSM_STATE, T * I)
        return [f(re), -f(im)]

    w_op = jnp.concatenate(wrows(cp_f, jnp.arange(1, T + 1)) + wrows(cp_b, T - jnp.arange(T)), axis=1)
    a1 = jnp.stack([a_f[0], a_f[0], a_b[0], a_b[0]]).reshape(4, G * SSM_STATE)
    a2 = jnp.stack([-a_f[1], a_f[1], -a_b[1], a_b[1]]).reshape(4, G * SSM_STATE)
    return (m_op, v_op.astype(jnp.bfloat16)), w_op.astype(jnp.bfloat16), a1, a2


def _s5_intra_body(u_ref, m_ref, v_ref, y_ref, z_ref):
    u = u_ref[0]
    y_ref[0] = jnp.dot(u, m_ref[0], preferred_element_type=jnp.float32)
    z_ref[0] = jnp.dot(u, v_ref[0], preferred_element_type=jnp.float32)


def _s5_intra(ug, mv):
    m_op, v_op = mv
    g, m, kdim = ug.shape
    n, nz = m_op.shape[-1], v_op.shape[-1]
    return pl.pallas_call(
        _s5_intra_body,
        grid=(g,),
        in_specs=[pl.BlockSpec((1, m, kdim), lambda i: (i, 0, 0)),
                  pl.BlockSpec((1, kdim, n), lambda i: (i, 0, 0)),
                  pl.BlockSpec((1, kdim, nz), lambda i: (i, 0, 0))],
        out_specs=[pl.BlockSpec((1, m, n), lambda i: (i, 0, 0)),
                   pl.BlockSpec((1, m, nz), lambda i: (i, 0, 0))],
        out_shape=[jax.ShapeDtypeStruct((g, m, n), jnp.float32),
                   jax.ShapeDtypeStruct((g, m, nz), jnp.float32)],
        compiler_params=_cparams(("arbitrary",)),
        name="s5_intra",
    )(ug, m_op, v_op)


def _s5_scan_body(z_ref, a1_ref, a2_ref, s_ref):
    steps = z_ref.shape[0]
    a1 = a1_ref[...]
    a2 = a2_ref[...]
    row = lax.broadcasted_iota(jnp.int32, a1.shape, 0)
    even = (row % 2) == 0

    def step(k, s):
        s_ref[k] = s
        partner = jnp.where(even, pltpu.roll(s, SUBLANES - 1, axis=0), pltpu.roll(s, 1, axis=0))
        return a1 * s + a2 * partner + z_ref[k]

    lax.fori_loop(0, steps, step, jnp.zeros(a1.shape, jnp.float32))


def _s5_scan(z8, a1, a2):
    steps, r, n = z8.shape
    cb = 512
    return pl.pallas_call(
        _s5_scan_body,
        grid=(n // cb,),
        in_specs=[pl.BlockSpec((steps, r, cb), lambda i: (0, 0, i)),
                  pl.BlockSpec((r, cb), lambda i: (0, i)),
                  pl.BlockSpec((r, cb), lambda i: (0, i))],
        out_specs=pl.BlockSpec((steps, r, cb), lambda i: (0, 0, i)),
        out_shape=jax.ShapeDtypeStruct((steps, r, n), jnp.float32),
        compiler_params=_cparams(("arbitrary",)),
        name="s5_scan",
    )(z8, a1, a2)


def _s5_out_body(y_ref, s_ref, w_ref, o_ref):
    o_ref[0] = (y_ref[0] + jnp.dot(s_ref[0], w_ref[0], preferred_element_type=jnp.float32)).astype(o_ref.dtype)


def _s5_out(yi, sg, w_op):
    g, m, n = yi.shape
    ks = sg.shape[-1]
    return pl.pallas_call(
        _s5_out_body,
        grid=(g,),
        in_specs=[pl.BlockSpec((1, m, n), lambda i: (i, 0, 0)),
                  pl.BlockSpec((1, m, ks), lambda i: (i, 0, 0)),
                  pl.BlockSpec((1, ks, n), lambda i: (i, 0, 0))],
        out_specs=pl.BlockSpec((1, m, n), lambda i: (i, 0, 0)),
        out_shape=jax.ShapeDtypeStruct((g, m, n), jnp.bfloat16),
        compiler_params=_cparams(("arbitrary",)),
        name="s5_out",
    )(yi, sg, w_op)


def _s5_mix(u, ops, nb, lb, lc):
    mv, w_op, a1, a2 = ops
    G, I, P, T = SSM_GROUPS, SSM_GROUP, SSM_STATE, S5_CHUNK
    nch = lb // T
    cch = lc // T
    ug = u.reshape(nb * nch, T, G, I).transpose(2, 0, 1, 3).reshape(G, nb * nch, T * I)
    yi, z = _s5_intra(ug, mv)
    order_b = jnp.concatenate([jnp.arange(cch - 1, -1, -1), jnp.arange(nch - 1, cch - 1, -1)])
    z6 = z.reshape(G, nb, nch, 2, 2, P)
    zf = z6[:, :, :, 0]
    zb = z6[:, :, order_b, 1]
    z8 = jnp.stack([zf, zb], axis=3)
    z8 = z8.transpose(2, 1, 3, 4, 0, 5).reshape(nch, nb * 4, G * P)
    rows = nb * 4
    pad = (-rows) % SUBLANES
    reps = (rows + pad) // 4
    if pad:
        z8 = jnp.pad(z8, ((0, 0), (0, pad), (0, 0)))
    s8 = _s5_scan(z8, jnp.tile(a1, (reps, 1)), jnp.tile(a2, (reps, 1)))[:, :rows]
    s6 = s8.reshape(nch, nb, 2, 2, G, P).transpose(4, 1, 0, 2, 3, 5)
    inv_b = jnp.argsort(order_b)
    sf = s6[:, :, :, 0]
    sb = s6[:, :, inv_b, 1]
    sg = jnp.stack([sf, sb], axis=3).reshape(G, nb * nch, 4 * P).astype(jnp.bfloat16)
    y = _s5_out(yi, sg, w_op)
    return y.reshape(G, nb * nch, T, I).transpose(1, 2, 0, 3).reshape(nb * lb, G * I)


def _ret_consts(ret_decay_f, ret_decay_b):
    f32 = jnp.float32
    C = RET_CHUNK
    lg_f = -jnp.exp(ret_decay_f.astype(f32))[:, None, None]
    lg_b = -jnp.exp(ret_decay_b.astype(f32))[:, None, None]
    pos = jnp.arange(C, dtype=f32)
    diff = pos[:, None] - pos[None, :]
    dmat = jnp.where(diff >= 0, jnp.exp(jnp.where(diff >= 0, diff, 0.0)[None] * lg_f),
                     jnp.exp(jnp.where(diff < 0, -diff, 0.0)[None] * lg_b))
    col = lambda e: jnp.broadcast_to(jnp.exp(e), (RET_HEADS, C, RET_HEAD_DIM))
    p1 = pos[None, :, None]
    qdec_f = col((p1 + 1.0) * lg_f)
    kdec_f = col((C - 1.0 - p1) * lg_f)
    qdec_b = col((C - p1) * lg_b)
    kdec_b = col(p1 * lg_b)
    cd_f = col(jnp.full_like(p1, C) * lg_f)
    cd_b = col(jnp.full_like(p1, C) * lg_b)
    return dmat, qdec_f, kdec_f, cd_f, qdec_b, kdec_b, cd_b


def _ret_state_update(s_ref, b, hd, kh, vh, kdec, cd):
    kd = (kh.astype(jnp.float32) * kdec).astype(jnp.bfloat16)
    inc = lax.dot_general(kd, vh, (((0,), (0,)), ((), ())), preferred_element_type=jnp.float32)
    s_ref[b, hd] = cd * s_ref[b, hd] + inc


def _ret_fwd_body(q_ref, k_ref, v_ref, dm_ref, qd_ref, kd_ref, cd_ref, o_ref, s_ref):
    @pl.when(pl.program_id(0) == 0)
    def _():
        s_ref[...] = jnp.zeros(s_ref.shape, s_ref.dtype)

    for b in range(q_ref.shape[0]):
        for hd in range(RET_HEADS):
            sl = slice(hd * RET_HEAD_DIM, (hd + 1) * RET_HEAD_DIM)
            qh, kh, vh = q_ref[b, :, sl], k_ref[b, :, sl], v_ref[b, :, sl]
            sc = lax.dot_general(qh, kh, (((1,), (1,)), ((), ())), preferred_element_type=jnp.float32)
            p = (sc * dm_ref[hd]).astype(jnp.bfloat16)
            o = jnp.dot(p, vh, preferred_element_type=jnp.float32)
            cross = jnp.dot(qh, s_ref[b, hd].astype(jnp.bfloat16), preferred_element_type=jnp.float32)
            o_ref[b, :, sl] = o + cross * qd_ref[hd]
            _ret_state_update(s_ref, b, hd, kh, vh, kd_ref[hd], cd_ref[hd])


def _ret_bwd_body(q_ref, k_ref, v_ref, of_ref, g_ref, qd_ref, kd_ref, cd_ref, y_ref, s_ref):
    @pl.when(pl.program_id(0) == 0)
    def _():
        s_ref[...] = jnp.zeros(s_ref.shape, s_ref.dtype)

    for b in range(q_ref.shape[0]):
        for hd in range(RET_HEADS):
            sl = slice(hd * RET_HEAD_DIM, (hd + 1) * RET_HEAD_DIM)
            qh, kh, vh = q_ref[b, :, sl], k_ref[b, :, sl], v_ref[b, :, sl]
            cross = jnp.dot(qh, s_ref[b, hd].astype(jnp.bfloat16), preferred_element_type=jnp.float32)
            o = of_ref[b, :, sl] + cross * qd_ref[hd]
            o = o * lax.rsqrt(jnp.mean(o * o, axis=-1, keepdims=True) + EPS)
            gv = g_ref[b, :, sl].astype(jnp.float32)
            y_ref[b, :, sl] = (o * (gv * jax.nn.sigmoid(gv))).astype(y_ref.dtype)
            _ret_state_update(s_ref, b, hd, kh, vh, kd_ref[hd], cd_ref[hd])


def _retention(q, k, v, g, consts, nb, lb, lc):
    dmat, qdec_f, kdec_f, cd_f, qdec_b, kdec_b, cd_b = consts
    C, W = RET_CHUNK, RET_WIDTH
    nblk, cblk = lb // C, lc // C
    r3 = lambda a: a.reshape(nb, lb, W)
    fwd_idx = lambda s: (0, s, 0)
    bwd_idx = lambda s: (0, jnp.where(s < cblk, cblk - 1 - s, nblk - 1 + cblk - s), 0)
    cst = pl.BlockSpec((RET_HEADS, C, RET_HEAD_DIM), lambda s: (0, 0, 0))
    state = pltpu.VMEM((nb, RET_HEADS, RET_HEAD_DIM, RET_HEAD_DIM), jnp.float32)
    o_f = pl.pallas_call(
        _ret_fwd_body,
        grid=(nblk,),
        in_specs=[pl.BlockSpec((nb, C, W), fwd_idx)] * 3 + [cst] * 4,
        out_specs=pl.BlockSpec((nb, C, W), fwd_idx),
        out_shape=jax.ShapeDtypeStruct((nb, lb, W), jnp.float32),
        scratch_shapes=[state],
        compiler_params=_cparams(("arbitrary",)),
        name="ret_fwd",
    )(r3(q), r3(k), r3(v), dmat, qdec_f, kdec_f, cd_f)
    y = pl.pallas_call(
        _ret_bwd_body,
        grid=(nblk,),
        in_specs=[pl.BlockSpec((nb, C, W), bwd_idx)] * 5 + [cst] * 3,
        out_specs=pl.BlockSpec((nb, C, W), bwd_idx),
        out_shape=jax.ShapeDtypeStruct((nb, lb, W), jnp.bfloat16),
        scratch_shapes=[state],
        compiler_params=_cparams(("arbitrary",)),
        name="ret_bwd",
    )(r3(q), r3(k), r3(v), o_f, r3(g), qdec_b, kdec_b, cd_b)
    return y.reshape(nb * lb, W)


def _gelu_tanh(x):
    return 0.5 * x * (1.0 + jnp.tanh(math.sqrt(2.0 / math.pi) * (x + 0.044715 * x * x * x)))


def _merge_body(x_ref, ys_ref, yr_ref, gs_ref, gr_ref, mod_ref, wg_ref, wr_ref, wo_ref, o_ref):
    f32 = jnp.float32
    a_in = _gelu_tanh(ys_ref[...].astype(f32)).astype(jnp.bfloat16)
    ab = jnp.dot(a_in, wg_ref[...], preferred_element_type=f32)
    ys = ab[:, :D_MODEL] * jax.nn.sigmoid(ab[:, D_MODEL:])
    yr = jnp.dot(yr_ref[...], wr_ref[...], preferred_element_type=f32)
    m = jax.nn.sigmoid(gs_ref[...].astype(f32)) * ys + jax.nn.sigmoid(gr_ref[...].astype(f32)) * yr
    y = jnp.dot(m.astype(jnp.bfloat16), wo_ref[...], preferred_element_type=f32)
    o_ref[...] = x_ref[...] + mod_ref[0][2:3] * y


def _merge(xs, ys, yr, gs, gr, modl, wg, wr, wo, blocks_per_batch, ctx_blocks):
    t, d = xs.shape
    tb = TOK_BLOCK
    row = lambda i: (i, 0)
    full = lambda a: pl.BlockSpec(a.shape, lambda i: (0, 0))
    return pl.pallas_call(
        _merge_body,
        grid=(t // tb,),
        in_specs=[pl.BlockSpec((tb, d), row), pl.BlockSpec((tb, SSM_WIDTH), row),
                  pl.BlockSpec((tb, RET_WIDTH), row), pl.BlockSpec((tb, d), row), pl.BlockSpec((tb, d), row),
                  pl.BlockSpec((1, 6, d), lambda i: (_mod_row(i, blocks_per_batch, ctx_blocks), 0, 0)),
                  full(wg), full(wr), full(wo)],
        out_specs=pl.BlockSpec((tb, d), row),
        out_shape=jax.ShapeDtypeStruct((t, d), jnp.float32),
        compiler_params=_cparams(("arbitrary",)),
        name="merge_out",
    )(xs, ys, yr, gs, gr, modl, wg, wr, wo)


def _top16_rows(vals, payload=None):
    n = vals.shape[0]
    rid = lax.broadcasted_iota(jnp.int32, vals.shape, 0).astype(jnp.float32)
    top_v, top_p = [], []
    for _ in range(PEER_TOPK):
        m = jnp.max(vals, axis=0, keepdims=True)
        first = jnp.min(jnp.where(vals == m, rid, float(n)), axis=0, keepdims=True)
        hit = rid == first
        top_v.append(m)
        top_p.append(first if payload is None else
                     jnp.max(jnp.where(hit, payload, -1.0), axis=0, keepdims=True))
        vals = jnp.where(hit, -jnp.inf, vals)
    return jnp.concatenate(top_v, axis=0), jnp.concatenate(top_p, axis=0)


def _pair_candidates(s1, i1, s2, i2):
    k = PEER_TOPK
    bid = lax.broadcasted_iota(jnp.int32, (SUBLANES, s1.shape[1]), 0)
    cs, ce = [s1[0:1] + s2], [i1[0:1] * float(PEER_NKEYS) + i2]
    for a in range(1, SUBLANES):
        keep = bid < (k // (a + 1))
        cs.append(jnp.where(keep, s1[a:a + 1] + s2[0:SUBLANES], -jnp.inf))
        ce.append(i1[a:a + 1] * float(PEER_NKEYS) + i2[0:SUBLANES])
    cs.append(s1[SUBLANES:k] + s2[0:1])
    ce.append(i1[SUBLANES:k] * float(PEER_NKEYS) + i2[0:1])
    return jnp.concatenate(cs, axis=0), jnp.concatenate(ce, axis=0)


def _route_body(x_ref, mod_ref, g_ref, wq_ref, keys_ref, h_ref, idx_ref, gate_ref):
    m = mod_ref[0]
    h2 = _norm_mod(x_ref[...], g_ref[...], m[3:4], m[4:5])
    h_ref[...] = h2
    q = jnp.dot(h2.astype(jnp.bfloat16), wq_ref[...], preferred_element_type=jnp.float32)
    idx_rows, gate_rows = [], []
    for hd in range(PEER_HEADS):
        qh = q[:, hd * PEER_DKEY:(hd + 1) * PEER_DKEY].astype(jnp.bfloat16)
        tops = []
        for s in range(2):
            st = lax.dot_general(keys_ref[hd, s], qh, (((1,), (1,)), ((), ())),
                                 preferred_element_type=jnp.float32)
            tops.append(_top16_rows(st))
        (s1, i1), (s2, i2) = tops
        best_s, best_e = _top16_rows(*_pair_candidates(s1, i1, s2, i2))
        ex = jnp.exp(best_s - best_s[0:1])
        gate_rows.append(ex / jnp.sum(ex, axis=0, keepdims=True))
        idx_rows.append(best_e)
    idx_ref[...] = jnp.concatenate(idx_rows, axis=0).T.astype(jnp.int32)
    gate_ref[...] = jnp.concatenate(gate_rows, axis=0).T


def _route(xs, modl, g2, wq_bf, keys_pad, blocks_per_batch, ctx_blocks):
    t, d = xs.shape
    tb = PEER_ROUTE_BLOCK
    row = lambda i: (i, 0)
    return pl.pallas_call(
        _route_body,
        grid=(t // tb,),
        in_specs=[pl.BlockSpec((tb, d), row),
                  pl.BlockSpec((1, 6, d), lambda i: (_mod_row(i, blocks_per_batch, ctx_blocks), 0, 0)),
                  pl.BlockSpec((1, d), lambda i: (0, 0)),
                  pl.BlockSpec((d, PEER_HEADS * PEER_DKEY), lambda i: (0, 0)),
                  pl.BlockSpec(keys_pad.shape, lambda i: (0, 0, 0, 0))],
        out_specs=[pl.BlockSpec((tb, d), row), pl.BlockSpec((tb, PICKS), row), pl.BlockSpec((tb, PICKS), row)],
        out_shape=[jax.ShapeDtypeStruct((t, d), jnp.float32),
                   jax.ShapeDtypeStruct((t, PICKS), jnp.int32), jax.ShapeDtypeStruct((t, PICKS), jnp.float32)],
        compiler_params=_cparams(("arbitrary",)),
        name="peer_route",
    )(xs, modl, g2, wq_bf, keys_pad)


def _pack_table(tab):
    n, d = tab.shape
    halves = tab.reshape(n, 2, d // 2 // LANES, LANES)
    bits = lax.bitcast_convert_type(halves.astype(jnp.bfloat16), jnp.uint16).astype(jnp.uint32)
    return lax.bitcast_convert_type((bits[:, 0] << 16) | bits[:, 1], jnp.int32)


_NT = (((1,), (1,)), ((), ()))
GATHER_PARTS = 4
IDX_OFFSETS = 8


def _hi_lo_rows(first, second, rid):
    f_hi = first.astype(jnp.bfloat16).astype(jnp.float32)
    s_hi = second.astype(jnp.bfloat16).astype(jnp.float32)
    v = jnp.where(rid == 0, f_hi, jnp.where(rid == 1, first - f_hi,
                  jnp.where(rid == 2, s_hi, jnp.where(rid == 3, second - s_hi, 0.0))))
    return v.astype(jnp.bfloat16)


def _gather_part(idx_ref, offs, tab_ref, st_ref, slot, t, part):
    n = PICKS // GATHER_PARTS
    nrow = tab_ref.shape[1]
    k = len(offs)
    for p0 in range(part * n, (part + 1) * n, 2 * k):
        sub = idx_ref.at[t, pl.ds(p0 // 2, k)]
        for j in range(k):
            word = sub[offs[j]]
            for h, e in enumerate((word & 0xFFFF, lax.shift_right_logical(word, 16))):
                st_ref[slot, pl.ds(p0 + 2 * j + h, nrow, stride=SLAB_STRIDE), :] = tab_ref[e]


def _slab(st_ref, slot, r):
    return pltpu.bitcast(st_ref[slot, r * SLAB_STRIDE:r * SLAB_STRIDE + PICKS, :], jnp.bfloat16)


def _token_pipeline(n_tok, gather_part, compute_part, finish):
    g = PEER_GROUP

    def step(slot, t, prev_slot, t_prev):
        acc = None
        for part in range(GATHER_PARTS):
            gather_part(slot, t, part)
            if prev_slot is not None:
                acc = compute_part(prev_slot, t_prev, part, acc)
        if prev_slot is not None:
            finish(t_prev, acc)

    step(0, 0, None, None)
    for j in range(1, g):
        step(j, j, j - 1, j - 1)

    def group(gi, carry):
        t0 = gi * g
        step(0, t0, g - 1, t0 - 1)
        for j in range(1, g):
            step(j, t0 + j, j - 1, t0 + j - 1)
        return carry

    lax.fori_loop(1, n_tok // g, group, 0)
    acc = None
    for part in range(GATHER_PARTS):
        acc = compute_part(g - 1, n_tok - 1, part, acc)
    finish(n_tok - 1, acc)


def _expert_act_body(idx_ref, off_ref, h_ref, gate_ref, tab_ref, w_ref, st_ref):
    rid = lax.broadcasted_iota(jnp.int32, (SUBLANES, LANES), 0)
    lane = lax.broadcasted_iota(jnp.int32, (1, 2 * PICKS), 1)
    even = (lane % 2) == 0

    def compute_part(slot, t, r, acc):
        half = h_ref.shape[1] // 2
        hrow = h_ref[pl.ds(t, 1), :]
        xh = jnp.broadcast_to(hrow[:, r * LANES:(r + 1) * LANES], rid.shape)
        xl = jnp.broadcast_to(hrow[:, half + r * LANES:half + (r + 1) * LANES], rid.shape)
        res = lax.dot_general(_hi_lo_rows(xh, xl, rid), _slab(st_ref, slot, r), _NT,
                              preferred_element_type=jnp.float32)
        return res if acc is None else acc + res

    def finish(t, acc):
        v = jnp.where(even, acc[2:3] + acc[3:4], acc[0:1] + acc[1:2])
        act = v + jnp.where(even, pltpu.roll(v, 2 * PICKS - 1, axis=1), pltpu.roll(v, 1, axis=1))
        gl = 0.5 * act * (1.0 + lax.erf(act * (1.0 / math.sqrt(2.0))))
        w_ref[pl.ds(t, 1), :] = gate_ref[pl.ds(t, 1), :] * gl

    offs = [off_ref[j] for j in range(IDX_OFFSETS)]
    _token_pipeline(h_ref.shape[0], partial(_gather_part, idx_ref, offs, tab_ref, st_ref), compute_part, finish)


def _expert_out_body(idx_ref, off_ref, w_ref, tab_ref, x_ref, mod_ref, o_ref, st_ref):
    rid = lax.broadcasted_iota(jnp.int32, (SUBLANES, 2 * PICKS), 0)
    lane = lax.broadcasted_iota(jnp.int32, (SUBLANES, 2 * PICKS), 1)
    even = (lane % 2) == 0

    def compute_part(slot, t, r, acc):
        wrow = jnp.broadcast_to(w_ref[pl.ds(t, 1), :], rid.shape)
        lhs = _hi_lo_rows(jnp.where(even, 0.0, wrow), jnp.where(even, wrow, 0.0), rid)
        res = jnp.dot(lhs, _slab(st_ref, slot, r), preferred_element_type=jnp.float32)
        first, second = acc if acc is not None else ([], [])
        return first + [res[0:1] + res[1:2]], second + [res[2:3] + res[3:4]]

    g2 = mod_ref[0][5:6]

    def finish(t, acc):
        o_ref[pl.ds(t, 1), :] = x_ref[pl.ds(t, 1), :] + g2 * jnp.concatenate(acc[0] + acc[1], axis=1)

    offs = [off_ref[j] for j in range(IDX_OFFSETS)]
    _token_pipeline(o_ref.shape[0], partial(_gather_part, idx_ref, offs, tab_ref, st_ref), compute_part, finish)


def _expert_specs(t):
    nt = PEER_TOK_BLOCK
    nrow = D_MODEL // 2 // LANES
    assert nrow == GATHER_PARTS and nt % PEER_GROUP == 0 and nt // PEER_GROUP >= 2
    smem = pl.BlockSpec((nt, PICKS // 2), lambda i: (i, 0), memory_space=pltpu.SMEM)
    vrow = pl.BlockSpec((nt, 2 * PICKS), lambda i: (i, 0))
    full = pl.BlockSpec((nt, D_MODEL), lambda i: (i, 0))
    table = pl.BlockSpec(memory_space=pltpu.VMEM)
    stage = pltpu.VMEM((PEER_GROUP, nrow * SLAB_STRIDE, LANES), jnp.int32)
    assert PICKS // GATHER_PARTS % (2 * IDX_OFFSETS) == 0
    offs = pl.BlockSpec(memory_space=pltpu.SMEM)
    return nt, smem, offs, vrow, full, table, stage


def _expert_act(idx, h2, gate2, tab_u):
    t = idx.shape[0]
    nt, smem, offs, vrow, full, table, stage = _expert_specs(t)
    return pl.pallas_call(
        _expert_act_body,
        grid=(t // nt,),
        in_specs=[smem, offs, full, vrow, table],
        out_specs=vrow,
        out_shape=jax.ShapeDtypeStruct((t, 2 * PICKS), jnp.float32),
        scratch_shapes=[stage],
        compiler_params=_cparams(("arbitrary",)),
        name="peer_act",
    )(idx, jnp.arange(IDX_OFFSETS, dtype=jnp.int32), h2, gate2, tab_u)


def _expert_out(idx, w2, tab_v, xs, modl, blocks_per_batch, ctx_blocks):
    t = idx.shape[0]
    nt, smem, offs, vrow, full, table, stage = _expert_specs(t)
    mod = pl.BlockSpec((1, 6, D_MODEL), lambda i: (_mod_row(i, blocks_per_batch, ctx_blocks), 0, 0))
    return pl.pallas_call(
        _expert_out_body,
        grid=(t // nt,),
        in_specs=[smem, offs, vrow, table, full, mod],
        out_specs=full,
        out_shape=jax.ShapeDtypeStruct((t, D_MODEL), jnp.float32),
        scratch_shapes=[stage],
        compiler_params=_cparams(("arbitrary",)),
        name="peer_out",
    )(idx, jnp.arange(IDX_OFFSETS, dtype=jnp.int32), w2, tab_v, xs, modl)


def _final_norm_body(x_ref, g_ref, o_ref):
    xf = x_ref[...]
    o_ref[...] = xf * lax.rsqrt(jnp.mean(xf * xf, axis=-1, keepdims=True) + EPS) * g_ref[...]


def _final_norm(x2, g):
    rows, d = x2.shape
    tm = TOK_BLOCK
    return pl.pallas_call(
        _final_norm_body,
        grid=(rows // tm,),
        in_specs=[pl.BlockSpec((tm, d), lambda i: (i, 0)), pl.BlockSpec((1, d), lambda i: (0, 0))],
        out_specs=pl.BlockSpec((tm, d), lambda i: (i, 0)),
        out_shape=jax.ShapeDtypeStruct((rows, d), x2.dtype),
        compiler_params=_cparams(("arbitrary",)),
        name="final_norm",
    )(x2, g.reshape(1, d))


def _rope_tables(seq, lc, nb):
    rows = seq // GRID_W
    row_ids = jnp.repeat(jnp.arange(rows, dtype=jnp.float32), GRID_W)
    col_ids = jnp.tile(jnp.arange(GRID_W, dtype=jnp.float32), rows)
    n_freq = RET_HEAD_DIM // 4
    freqs = ROPE_BASE ** (-jnp.arange(n_freq, dtype=jnp.float32) / n_freq)
    ang_r, ang_c = row_ids[:, None] * freqs[None], col_ids[:, None] * freqs[None]
    cos = jnp.concatenate([jnp.cos(ang_r), jnp.cos(ang_r), jnp.cos(ang_c), jnp.cos(ang_c)], axis=-1)
    sin = jnp.concatenate([-jnp.sin(ang_r), jnp.sin(ang_r), -jnp.sin(ang_c), jnp.sin(ang_c)], axis=-1)
    cos = jnp.concatenate([jnp.ones((lc, RET_HEAD_DIM), jnp.float32), cos], axis=0)
    sin = jnp.concatenate([jnp.zeros((lc, RET_HEAD_DIM), jnp.float32), sin], axis=0)
    return jnp.tile(cos, (nb, 1)), jnp.tile(sin, (nb, 1))


def kernel(x, c, ctx, c_ctx, w_mod, b_mod, norm1_g, norm2_g, w_in, ssm_B_re, ssm_B_im,
           ssm_C_re, ssm_C_im, ssm_D, ssm_lam_re_f, ssm_lam_im_f, ssm_log_dt_f,
           ssm_lam_re_b, ssm_lam_im_b, ssm_log_dt_b, w_ssm_glu, ret_decay_f, ret_decay_b,
           w_ret_up, w_out, peer_w_q, peer_sub_keys, peer_u, peer_v, final_norm_g):
    nb, seq, d = x.shape
    lc = ctx.shape[1]
    depth = w_mod.shape[0]
    lb = lc + seq
    bf = jnp.bfloat16
    assert nb == 2 and d == D_MODEL and lc % TOK_BLOCK == 0 and seq % TOK_BLOCK == 0
    bpb, cbl = lb // TOK_BLOCK, lc // TOK_BLOCK
    bpb_r, cbl_r = lb // PEER_ROUTE_BLOCK, lc // PEER_ROUTE_BLOCK

    xs = jnp.concatenate([ctx, x], axis=1).reshape(nb * lb, d)
    cvec = jnp.zeros((SUBLANES, d), jnp.float32).at[:nb].set(c).at[nb].set(c_ctx)
    mod_all = _modulation(cvec, w_mod, b_mod)[:, :nb + 1].reshape(depth, nb + 1, 6, d)
    cos_t, sin_t = _rope_tables(seq, lc, nb)
    half = PEER_DKEY // 2

    for i in range(depth):
        modl = mod_all[i]
        s5_ops = _s5_operators(ssm_B_re[i], ssm_B_im[i], ssm_C_re[i], ssm_C_im[i], ssm_D[i],
                               ssm_lam_re_f[i], ssm_lam_im_f[i], ssm_log_dt_f[i],
                               ssm_lam_re_b[i], ssm_lam_im_b[i], ssm_log_dt_b[i])
        ret_c = _ret_consts(ret_decay_f[i], ret_decay_b[i])
        u, q, k, v, g, gs, gr = _in_proj(xs, modl, norm1_g[i].reshape(1, d), w_in[i].astype(bf),
                                         cos_t, sin_t, bpb, cbl)
        ys = _s5_mix(u, s5_ops, nb, lb, lc)
        yr = _retention(q, k, v, g, ret_c, nb, lb, lc)
        xs = _merge(xs, ys, yr, gs, gr, modl, w_ssm_glu[i].astype(bf), w_ret_up[i].astype(bf),
                    w_out[i].astype(bf), bpb, cbl)
        sk = peer_sub_keys[i].astype(bf)
        zeros = jnp.zeros_like(sk)
        keys_pad = jnp.stack([jnp.concatenate([sk[:, 0], zeros[:, 0]], axis=-1),
                              jnp.concatenate([zeros[:, 1], sk[:, 1]], axis=-1)], axis=1)
        h2, idx, gate = _route(xs, modl, norm2_g[i].reshape(1, d), peer_w_q[i].astype(bf),
                                   keys_pad, bpb_r, cbl_r)
        idx2 = idx[:, 0::2] | (idx[:, 1::2] << 16)
        wts = _expert_act(idx2, h2, jnp.repeat(gate, 2, axis=1), _pack_table(peer_u[i]))
        xs = _expert_out(idx2, wts, _pack_table(peer_v[i]), xs, modl, lb // PEER_TOK_BLOCK, lc // PEER_TOK_BLOCK)

    lat = xs.reshape(nb, lb, d)[:, lc:].reshape(nb * seq, d)
    return _final_norm(lat, final_norm_g).reshape(nb, seq, d)
```
